```python
import math
import jax
import jax.numpy as jnp
from jax import lax
import numpy as np

D_MODEL = 1024
BATCH = 1
SEQ = 16384
DEPTH = 2
DEC_BATCH = 16
DEC_SEQ = 32
PAST_LEN = 2048

CHUNK = 64
N_A_LAYERS = DEPTH // 2
N_B_LAYERS = DEPTH - N_A_LAYERS
D_FF = 2816
CONV_CHANNELS = D_MODEL
CONV_WIDTH = 31
CONV_HIST = CONV_WIDTH - 1
N_HEADS = 8
HEAD_DIM = D_MODEL // N_HEADS // 2
QK_DIM = N_HEADS * 2 * HEAD_DIM
V_DIM = N_HEADS * 2 * HEAD_DIM
Q_BLOCK = 128
EPS = 1e-6
SCALE = 1.0 / math.sqrt(HEAD_DIM)

kernel_name = "yoco_conformer_conv_diff_attn_step"


def rms_norm(x, g):
    xf = x.astype(jnp.float32)
    y = xf * lax.rsqrt(jnp.mean(xf * xf, axis=-1, keepdims=True) + EPS)
    return (y * g.astype(jnp.float32)).astype(x.dtype)


def layer_norm(x, g, b):
    xf = x.astype(jnp.float32)
    mu = jnp.mean(xf, axis=-1, keepdims=True)
    var = jnp.mean(jnp.square(xf - mu), axis=-1, keepdims=True)
    y = (xf - mu) * lax.rsqrt(var + EPS)
    return (y * g.astype(jnp.float32) + b.astype(jnp.float32)).astype(x.dtype)


def half_ffn(x, g, w_gate, w_up, w_down):
    h = rms_norm(x, g)
    return x + 0.5 * ((jax.nn.silu(h @ w_gate) * (h @ w_up)) @ w_down)


def conv_module(h, hist, w_pw1, b_pw1, w_dw, b_dw, ln_g, ln_b, w_pw2, b_pw2):
    u = h @ w_pw1 + b_pw1
    a, gate = jnp.split(u, 2, axis=-1)
    u = a * jax.nn.sigmoid(gate)
    full = jnp.concatenate([hist.astype(u.dtype), u], axis=1)
    c = lax.conv_general_dilated(
        full, w_dw[:, None, :].astype(u.dtype), window_strides=(1,), padding='VALID',
        dimension_numbers=('NWC', 'WIO', 'NWC'), feature_group_count=CONV_CHANNELS) + b_dw
    c = jax.nn.silu(layer_norm(c, ln_g, ln_b))
    return c @ w_pw2 + b_pw2, full[:, -CONV_HIST:]


def diff_lambda(lq1, lk1, lq2, lk2, lambda_init):
    f32 = jnp.float32
    return (jnp.exp(jnp.sum(lq1.astype(f32) * lk1.astype(f32)))
            - jnp.exp(jnp.sum(lq2.astype(f32) * lk2.astype(f32))) + lambda_init)


def diff_weights(q, k, lam, mask=None):
    s = jnp.einsum('bqhmd,bkhmd->bhmqk', q, k, preferred_element_type=jnp.float32) * SCALE
    if mask is not None:
        s = jnp.where(mask, s, -jnp.inf)
    p = jax.nn.softmax(s, axis=-1)
    return p[:, :, 0] - lam * p[:, :, 1]


def attend_prompt(q, k, v, lam):
    b, s = q.shape[0], q.shape[1]
    nb = s // Q_BLOCK
    qb = q.reshape(b, nb, Q_BLOCK, N_HEADS, 2, HEAD_DIM).swapaxes(0, 1)
    key_chunk = jnp.arange(s) // CHUNK

    def block(args):
        qi, i = args
        q_chunk = (i * Q_BLOCK + jnp.arange(Q_BLOCK)) // CHUNK
        mask = key_chunk[None, :] <= q_chunk[:, None]
        w = diff_weights(qi, k, lam, mask)
        return jnp.einsum('bhqk,bkhe->bqhe', w.astype(v.dtype), v)

    o = lax.map(block, (qb, jnp.arange(nb)))
    return o.swapaxes(0, 1).reshape(b, s, N_HEADS, 2 * HEAD_DIM)


def attend_sample(q, k, v, lam):
    w = diff_weights(q, k, lam)
    return jnp.einsum('bhqk,bkhe->bqhe', w.astype(v.dtype), v)


def trunk(x, conv_hist, past_k, past_v, p):
    b, t, _ = x.shape
    new_hist = []
    k_new = v_new = k_all = v_all = None
    for layer in range(DEPTH):
        x = half_ffn(x, p['ffn_norm'][layer, 0], p['ffn_w_gate'][layer, 0],
                     p['ffn_w_up'][layer, 0], p['ffn_w_down'][layer, 0])
        if layer < N_A_LAYERS:
            i = layer
            h = rms_norm(x, p['conv_norm'][i])
            out, hist = conv_module(h, conv_hist[i], p['conv_w_pw1'][i], p['conv_b_pw1'][i],
                                    p['conv_w_dw'][i], p['conv_b_dw'][i], p['conv_ln_g'][i],
                                    p['conv_ln_b'][i], p['conv_w_pw2'][i], p['conv_b_pw2'][i])
            x = x + out
            new_hist.append(hist)
        else:
            j = layer - N_A_LAYERS
            lambda_init = 0.8 - 0.6 * math.exp(-0.3 * layer)
            h = rms_norm(x, p['attn_norm'][j])
            q = rms_norm((h @ p['w_q'][j]).reshape(b, t, N_HEADS, 2, HEAD_DIM), p['q_norm'][j])
            lam = diff_lambda(p['lambda_q1'][j], p['lambda_k1'][j],
                              p['lambda_q2'][j], p['lambda_k2'][j], lambda_init)
            if past_k is None:
                o = attend_prompt(q, k_new, v_new, lam)
            else:
                o = attend_sample(q, k_all, v_all, lam)
            o = rms_norm(o, p['subln'][j]) * (1.0 - lambda_init)
            x = x + o.reshape(b, t, V_DIM) @ p['w_o'][j]
        x = half_ffn(x, p['ffn_norm'][layer, 1], p['ffn_w_gate'][layer, 1],
                     p['ffn_w_up'][layer, 1], p['ffn_w_down'][layer, 1])
        if layer == N_A_LAYERS - 1:
            h = rms_norm(x, p['kv_norm'])
            k_new = rms_norm((h @ p['w_k']).reshape(b, t, N_HEADS, 2, HEAD_DIM), p['k_norm'])
            v_new = (h @ p['w_v']).reshape(b, t, N_HEADS, 2 * HEAD_DIM)
            if past_k is not None:
                k_all = jnp.concatenate([past_k.astype(k_new.dtype), k_new], axis=1)
                v_all = jnp.concatenate([past_v.astype(v_new.dtype), v_new], axis=1)
    return x, k_new, v_new, jnp.stack(new_hist)


def setup_inputs(seed: int = 0) -> dict:
    key = jax.random.key(seed)
    ks = iter(jax.random.split(key, 40))
    nrm = lambda shape, scale: jax.random.normal(next(ks), shape, jnp.float32) * scale
    gain = lambda shape: 1.0 + nrm(shape, 0.01)
    return {
        "x_prompt": nrm((BATCH, SEQ, D_MODEL), 1.0),
        "x_sample": nrm((DEC_BATCH, DEC_SEQ, D_MODEL), 1.0),
        "cache_k": nrm((DEC_BATCH, PAST_LEN, N_HEADS, 2, HEAD_DIM), 1.0),
        "cache_v": nrm((DEC_BATCH, PAST_LEN, N_HEADS, 2 * HEAD_DIM), 1.0),
        "state_conv": nrm((N_A_LAYERS, DEC_BATCH, CONV_HIST, CONV_CHANNELS), 1.0),
        "ffn_norm": gain((DEPTH, 2, D_MODEL)),
        "ffn_w_gate": nrm((DEPTH, 2, D_MODEL, D_FF), D_MODEL ** -0.5),
        "ffn_w_up": nrm((DEPTH, 2, D_MODEL, D_FF), D_MODEL ** -0.5),
        "ffn_w_down": nrm((DEPTH, 2, D_FF, D_MODEL), D_FF ** -0.5),
        "conv_norm": gain((N_A_LAYERS, D_MODEL)),
        "conv_w_pw1": nrm((N_A_LAYERS, D_MODEL, 2 * CONV_CHANNELS), D_MODEL ** -0.5),
        "conv_b_pw1": nrm((N_A_LAYERS, 2 * CONV_CHANNELS), 0.01),
        "conv_w_dw": nrm((N_A_LAYERS, CONV_WIDTH, CONV_CHANNELS), CONV_WIDTH ** -0.5),
        "conv_b_dw": nrm((N_A_LAYERS, CONV_CHANNELS), 0.01),
        "conv_ln_g": gain((N_A_LAYERS, CONV_CHANNELS)),
        "conv_ln_b": nrm((N_A_LAYERS, CONV_CHANNELS), 0.01),
        "conv_w_pw2": nrm((N_A_LAYERS, CONV_CHANNELS, D_MODEL), CONV_CHANNELS ** -0.5),
        "conv_b_pw2": nrm((N_A_LAYERS, D_MODEL), 0.01),
        "kv_norm": gain((D_MODEL,)),
        "w_k": nrm((D_MODEL, QK_DIM), D_MODEL ** -0.5),
        "w_v": nrm((D_MODEL, V_DIM), D_MODEL ** -0.5),
        "k_norm": gain((HEAD_DIM,)),
        "attn_norm": gain((N_B_LAYERS, D_MODEL)),
        "w_q": nrm((N_B_LAYERS, D_MODEL, QK_DIM), D_MODEL ** -0.5),
        "q_norm": gain((N_B_LAYERS, HEAD_DIM)),
        "lambda_q1": nrm((N_B_LAYERS, HEAD_DIM), 0.1),
        "lambda_k1": nrm((N_B_LAYERS, HEAD_DIM), 0.1),
        "lambda_q2": nrm((N_B_LAYERS, HEAD_DIM), 0.1),
        "lambda_k2": nrm((N_B_LAYERS, HEAD_DIM), 0.1),
        "subln": gain((N_B_LAYERS, 2 * HEAD_DIM)),
        "w_o": nrm((N_B_LAYERS, V_DIM, D_MODEL), V_DIM ** -0.5),
    }


def reference(x_prompt, x_sample, cache_k, cache_v, state_conv,
              ffn_norm, ffn_w_gate, ffn_w_up, ffn_w_down,
              conv_norm, conv_w_pw1, conv_b_pw1, conv_w_dw, conv_b_dw,
              conv_ln_g, conv_ln_b, conv_w_pw2, conv_b_pw2,
              kv_norm, w_k, w_v, k_norm,
              attn_norm, w_q, q_norm, lambda_q1, lambda_k1, lambda_q2, lambda_k2,
              subln, w_o):
    p = dict(ffn_norm=ffn_norm, ffn_w_gate=ffn_w_gate, ffn_w_up=ffn_w_up, ffn_w_down=ffn_w_down,
             conv_norm=conv_norm, conv_w_pw1=conv_w_pw1, conv_b_pw1=conv_b_pw1,
             conv_w_dw=conv_w_dw, conv_b_dw=conv_b_dw, conv_ln_g=conv_ln_g, conv_ln_b=conv_ln_b,
             conv_w_pw2=conv_w_pw2, conv_b_pw2=conv_b_pw2,
             kv_norm=kv_norm, w_k=w_k, w_v=w_v, k_norm=k_norm,
             attn_norm=attn_norm, w_q=w_q, q_norm=q_norm,
             lambda_q1=lambda_q1, lambda_k1=lambda_k1, lambda_q2=lambda_q2, lambda_k2=lambda_k2,
             subln=subln, w_o=w_o)
    zero_hist = jnp.zeros((N_A_LAYERS, x_prompt.shape[0], CONV_HIST, CONV_CHANNELS), x_prompt.dtype)
    y_prompt, k_prompt, v_prompt, conv_prompt = trunk(x_prompt, zero_hist, None, None, p)
    y_sample, k_sample, v_sample, conv_sample = trunk(x_sample, state_conv, cache_k, cache_v, p)
    return (y_prompt, y_sample, k_prompt, v_prompt, conv_prompt, k_sample, v_sample, conv_sample)
```

```python
import functools
import math

import jax
import jax.numpy as jnp
from jax import lax
from jax.experimental import pallas as pl
from jax.experimental.pallas import tpu as pltpu

D_MODEL = 1024
D_FF = 2816
CONV_C = 1024
CONV_W = 31
CONV_HIST = CONV_W - 1
N_HEADS = 8
HEAD_DIM = 64
HEAD_W = 2 * HEAD_DIM
CHUNK = 64
EPS = 1e-6
SCALE = 1.0 / math.sqrt(HEAD_DIM)
NEG = -1e30

F32 = jnp.float32
BF16 = jnp.bfloat16

SUBLANES = 8
MXU_DIM = 256
ROW_TILE = 512
ATTN_TILE = 512
CONV_PAD = 32
VMEM_LIMIT_BYTES = 56 * 1024 * 1024


def _params(n_grid_dims):
    return pltpu.CompilerParams(
        dimension_semantics=("arbitrary",) * n_grid_dims,
        vmem_limit_bytes=VMEM_LIMIT_BYTES)


def _const_spec(shape):
    zeros = (0,) * len(shape)
    return pl.BlockSpec(shape, lambda *_: zeros, pipeline_mode=pl.Buffered(1))


def _rows_spec(tm, width):
    return pl.BlockSpec((tm, width), lambda i: (i, 0))


def _dot(a, b):
    return jnp.dot(a, b, preferred_element_type=F32)


def _rms(x, g):
    return x * lax.rsqrt(jnp.mean(x * x, axis=-1, keepdims=True) + EPS) * g


def _group_rms(y, g_tiled):
    r = lax.broadcasted_iota(jnp.int32, (MXU_DIM, MXU_DIM), 0) // HEAD_DIM
    c = lax.broadcasted_iota(jnp.int32, (MXU_DIM, MXU_DIM), 1) // HEAD_DIM
    ones_bd = (r == c).astype(BF16)
    outs = []
    for j in range(y.shape[1] // MXU_DIM):
        ys = y[:, j * MXU_DIM:(j + 1) * MXU_DIM]
        ss = _dot((ys * ys).astype(BF16), ones_bd)
        outs.append(ys * lax.rsqrt(ss * (1.0 / HEAD_DIM) + EPS))
    return jnp.concatenate(outs, axis=1) * g_tiled


def _ffn_body(x, g_ref, wg_ref, wu_ref, wd_ref):
    h = _rms(x, g_ref[...]).astype(BF16)
    gate = _dot(h, wg_ref[...])
    up = _dot(h, wu_ref[...])
    act = (gate * jax.nn.sigmoid(gate) * up).astype(BF16)
    return x + 0.5 * _dot(act, wd_ref[...])


def _ffn_kernel(x_ref, g_ref, wg_ref, wu_ref, wd_ref, o_ref):
    o_ref[...] = _ffn_body(x_ref[...], g_ref, wg_ref, wu_ref, wd_ref)


def _proj_ffn_kernel(x_ref, a_ref, wo_ref, g_ref, wg_ref, wu_ref, wd_ref, o_ref):
    x = x_ref[...] + _dot(a_ref[...], wo_ref[...])
    o_ref[...] = _ffn_body(x, g_ref, wg_ref, wu_ref, wd_ref)


def _ffn(x, g, wg, wu, wd, attn=None, wo=None):
    rows = x.shape[0]
    tm = min(ROW_TILE, rows)
    w_specs = [_const_spec((1, D_MODEL)), _const_spec((D_MODEL, D_FF)),
               _const_spec((D_MODEL, D_FF)), _const_spec((D_FF, D_MODEL))]
    if attn is None:
        body, ins = _ffn_kernel, (x, g, wg, wu, wd)
        in_specs = [_rows_spec(tm, D_MODEL)] + w_specs
    else:
        body, ins = _proj_ffn_kernel, (x, attn, wo, g, wg, wu, wd)
        in_specs = [_rows_spec(tm, D_MODEL), _rows_spec(tm, D_MODEL),
                    _const_spec((D_MODEL, D_MODEL))] + w_specs
    return pl.pallas_call(
        body, grid=(rows // tm,), in_specs=in_specs,
        out_specs=_rows_spec(tm, D_MODEL),
        out_shape=jax.ShapeDtypeStruct((rows, D_MODEL), F32),
        compiler_params=_params(1), name="ffn" if attn is None else "proj_ffn",
    )(*ins)


def _conv_kernel(x_ref, hist_ref, g_ref, w1_ref, b1_ref, wdw_ref, bdw_ref,
                 lng_ref, lnb_ref, w2_ref, b2_ref, o_ref, hist_out_ref,
                 shift_ref, c_ref, *, nb, t, carry):
    i = pl.program_id(0)
    length = CONV_PAD + t
    x = x_ref[...]
    h = _rms(x, g_ref[...]).astype(BF16)
    u = _dot(h, w1_ref[...]) + b1_ref[...]
    glu = u[:, :CONV_C] * jax.nn.sigmoid(u[:, CONV_C:])

    lead = CONV_PAD - CONV_HIST
    if carry:
        @pl.when(i == 0)
        def _():
            shift_ref[0, :, lead:CONV_PAD, :] = hist_ref[...]
    else:
        shift_ref[0, :, lead:CONV_PAD, :] = hist_ref[...]
    shift_ref[0, :, 0:lead, :] = jnp.zeros((nb, lead, CONV_C), F32)
    shift_ref[0, :, CONV_PAD:, :] = glu.reshape(nb, t, CONV_C)
    for s in range(1, SUBLANES):
        shift_ref[s, :, 0:length - SUBLANES, :] = shift_ref[0, :, s:s + length - SUBLANES, :]

    def conv_rows(b, j):
        r0 = pl.multiple_of(j * SUBLANES, SUBLANES)
        acc = jnp.broadcast_to(bdw_ref[...], (SUBLANES, CONV_C))
        for w in range(CONV_W):
            off = lead + w
            s = off % SUBLANES
            tap = shift_ref[s, b, pl.ds(r0 + (off - s), SUBLANES), :]
            acc = acc + tap * wdw_ref[w:w + 1, :]
        c_ref[pl.ds(pl.multiple_of(b * t + r0, SUBLANES), SUBLANES), :] = acc

    def conv_stream(b, carry_val):
        def conv_step(j, c):
            conv_rows(b, j)
            return c
        return lax.fori_loop(0, t // SUBLANES, conv_step, carry_val)

    lax.fori_loop(0, nb, conv_stream, 0)

    c = c_ref[...]
    mu = jnp.mean(c, axis=-1, keepdims=True)
    cc = c - mu
    var = jnp.mean(cc * cc, axis=-1, keepdims=True)
    y = cc * lax.rsqrt(var + EPS) * lng_ref[...] + lnb_ref[...]
    y = (y * jax.nn.sigmoid(y)).astype(BF16)
    o_ref[...] = x + _dot(y, w2_ref[...]) + b2_ref[...]

    tail = shift_ref[0, :, length - CONV_HIST:length, :]
    hist_out_ref[...] = tail
    if carry:
        shift_ref[0, :, lead:CONV_PAD, :] = tail


def _conv(x, hist, g, w1, b1, wdw, bdw, lng, lnb, w2, b2, *, nb, t):
    rows = x.shape[0]
    n_streams = hist.shape[0]
    tm = nb * t
    carry = nb == 1 and rows // n_streams > t
    if carry:
        assert n_streams == 1
        hist_map = lambda i: (0, 0, 0)
    else:
        assert rows == n_streams * t
        hist_map = lambda i: (i, 0, 0)
    vec = lambda n: _const_spec((1, n))
    return pl.pallas_call(
        functools.partial(_conv_kernel, nb=nb, t=t, carry=carry),
        grid=(rows // tm,),
        in_specs=[_rows_spec(tm, D_MODEL),
                  pl.BlockSpec((nb, CONV_HIST, CONV_C), hist_map),
                  vec(D_MODEL), _const_spec((D_MODEL, 2 * CONV_C)), vec(2 * CONV_C),
                  _const_spec((CONV_W, CONV_C)), vec(CONV_C), vec(CONV_C), vec(CONV_C),
                  _const_spec((CONV_C, D_MODEL)), vec(D_MODEL)],
        out_specs=[_rows_spec(tm, D_MODEL),
                   pl.BlockSpec((nb, CONV_HIST, CONV_C), hist_map)],
        out_shape=[jax.ShapeDtypeStruct((rows, D_MODEL), F32),
                   jax.ShapeDtypeStruct((n_streams, CONV_HIST, CONV_C), F32)],
        scratch_shapes=[pltpu.VMEM((SUBLANES, nb, CONV_PAD + t, CONV_C), F32),
                        pltpu.VMEM((tm, CONV_C), F32)],
        compiler_params=_params(1), name="conv_module",
    )(x, hist, g, w1, b1, wdw, bdw, lng, lnb, w2, b2)


def _kv_kernel(x_ref, g_ref, wk_ref, wv_ref, kg_ref, k_ref, v_ref, kb_ref, vt_ref):
    h = _rms(x_ref[...], g_ref[...]).astype(BF16)
    k = _group_rms(_dot(h, wk_ref[...]), kg_ref[...])
    v = _dot(h, wv_ref[...])
    k_ref[...] = k
    v_ref[...] = v
    kb_ref[...] = k.astype(BF16)
    tm = v.shape[0]
    vt_ref[:, 0] = v.T.astype(BF16).reshape(N_HEADS, HEAD_W, tm)


def _kv(x, g, wk, wv, kg):
    rows = x.shape[0]
    tm = min(ROW_TILE, rows)
    n = rows // tm
    qk = N_HEADS * HEAD_W
    return pl.pallas_call(
        _kv_kernel, grid=(n,),
        in_specs=[_rows_spec(tm, D_MODEL), _const_spec((1, D_MODEL)),
                  _const_spec((D_MODEL, qk)), _const_spec((D_MODEL, qk)),
                  _const_spec((1, qk))],
        out_specs=[_rows_spec(tm, qk), _rows_spec(tm, qk), _rows_spec(tm, qk),
                   pl.BlockSpec((N_HEADS, 1, HEAD_W, tm), lambda i: (0, i, 0, 0))],
        out_shape=[jax.ShapeDtypeStruct((rows, qk), F32),
                   jax.ShapeDtypeStruct((rows, qk), F32),
                   jax.ShapeDtypeStruct((rows, qk), BF16),
                   jax.ShapeDtypeStruct((N_HEADS, n, HEAD_W, tm), BF16)],
        compiler_params=_params(1), name="kv_proj",
    )(x, g, wk, wv, kg)


def _q_kernel(x_ref, g_ref, wq_ref, qg_ref, q_ref):
    h = _rms(x_ref[...], g_ref[...]).astype(BF16)
    q = _group_rms(_dot(h, wq_ref[...]), qg_ref[...]) * SCALE
    q_ref[...] = q.astype(BF16)


def _q(x, g, wq, qg):
    rows = x.shape[0]
    tm = min(ROW_TILE, rows)
    qk = N_HEADS * HEAD_W
    return pl.pallas_call(
        _q_kernel, grid=(rows // tm,),
        in_specs=[_rows_spec(tm, D_MODEL), _const_spec((1, D_MODEL)),
                  _const_spec((D_MODEL, qk)), _const_spec((1, qk))],
        out_specs=_rows_spec(tm, qk),
        out_shape=jax.ShapeDtypeStruct((rows, qk), BF16),
        compiler_params=_params(1), name="q_proj",
    )(x, g, wq, qg)


def _lambda(lq1_ref, lk1_ref, lq2_ref, lk2_ref, lam_init):
    s1 = jnp.sum(lq1_ref[...] * lk1_ref[...], axis=-1, keepdims=True)
    s2 = jnp.sum(lq2_ref[...] * lk2_ref[...], axis=-1, keepdims=True)
    return jnp.exp(s1) - jnp.exp(s2) + lam_init


def _attn_prompt_kernel(q_ref, k_ref, vt_ref, lq1_ref, lk1_ref, lq2_ref, lk2_ref,
                        sg_ref, o_ref, qbd_ref, m_ref, l_ref, acc_ref, *, tile, lam_init):
    qi = pl.program_id(1)
    q_t = q_ref[...].astype(F32).T
    row = lax.broadcasted_iota(jnp.int32, (HEAD_W, 2 * tile), 0)
    col = lax.broadcasted_iota(jnp.int32, (HEAD_W, 2 * tile), 1)
    qq = jnp.concatenate([q_t, q_t], axis=1)
    qbd_ref[...] = jnp.where((row < HEAD_DIM) == (col < tile), qq, 0.0).astype(BF16)
    m_ref[...] = jnp.full(m_ref.shape, NEG, F32)
    l_ref[...] = jnp.zeros(l_ref.shape, F32)
    acc_ref[...] = jnp.zeros(acc_ref.shape, F32)

    def scores(j):
        kb = k_ref[pl.ds(pl.multiple_of(j * tile, tile), tile), :]
        return _dot(kb, qbd_ref[...])

    def update(s, j):
        m_old = m_ref[...]
        m_new = jnp.maximum(m_old, jnp.max(s, axis=0, keepdims=True))
        alpha = jnp.exp(m_old - m_new)
        p = jnp.exp(s - m_new)
        l_ref[...] = alpha * l_ref[...] + jnp.sum(p, axis=0, keepdims=True)
        acc_ref[...] = acc_ref[...] * alpha + _dot(vt_ref[0, j], p.astype(BF16))
        m_ref[...] = m_new

    def full_block(j, c):
        update(scores(j), j)
        return c

    lax.fori_loop(0, qi, full_block, 0)

    key_chunk = lax.broadcasted_iota(jnp.int32, (tile, 2 * tile), 0) // CHUNK
    qry = lax.broadcasted_iota(jnp.int32, (tile, 2 * tile), 1)
    qry_chunk = jnp.where(qry >= tile, qry - tile, qry) // CHUNK
    update(jnp.where(key_chunk <= qry_chunk, scores(qi), NEG), qi)

    a = acc_ref[...] / l_ref[...]
    lam = _lambda(lq1_ref, lk1_ref, lq2_ref, lk2_ref, lam_init)
    o_t = a[:, :tile] - lam * a[:, tile:]
    ms = jnp.mean(o_t * o_t, axis=0, keepdims=True)
    o_t = o_t * lax.rsqrt(ms + EPS) * sg_ref[...] * (1.0 - lam_init)
    o_ref[...] = o_t.T.astype(BF16)


def _attn_prompt(q, kb, vt, lams, sg_col, lam_init):
    seq = q.shape[0]
    tile = ATTN_TILE
    n = seq // tile
    lam_spec = pl.BlockSpec((1, HEAD_DIM), lambda h, i: (0, 0))
    return pl.pallas_call(
        functools.partial(_attn_prompt_kernel, tile=tile, lam_init=lam_init),
        grid=(N_HEADS, n),
        in_specs=[pl.BlockSpec((tile, HEAD_W), lambda h, i: (i, h)),
                  pl.BlockSpec((seq, HEAD_W), lambda h, i: (0, h)),
                  pl.BlockSpec((1, n, HEAD_W, tile), lambda h, i: (h, 0, 0, 0)),
                  lam_spec, lam_spec, lam_spec, lam_spec,
                  pl.BlockSpec((HEAD_W, 1), lambda h, i: (0, 0))],
        out_specs=pl.BlockSpec((tile, HEAD_W), lambda h, i: (i, h)),
        out_shape=jax.ShapeDtypeStruct((seq, N_HEADS * HEAD_W), BF16),
        scratch_shapes=[pltpu.VMEM((HEAD_W, 2 * tile), BF16),
                        pltpu.VMEM((1, 2 * tile), F32),
                        pltpu.VMEM((1, 2 * tile), F32),
                        pltpu.VMEM((HEAD_W, 2 * tile), F32)],
        compiler_params=_params(2), name="attn_prompt",
    )(q, kb, vt, *lams, sg_col)


def _attn_sample_kernel(q_ref, ck_ref, cv_ref, kn_ref, vn_ref, lq1_ref, lk1_ref,
                        lq2_ref, lk2_ref, sg_ref, o_ref, *, t, lam_init):
    q = q_ref[...]
    row = lax.broadcasted_iota(jnp.int32, (2 * t, HEAD_W), 0)
    col = lax.broadcasted_iota(jnp.int32, (2 * t, HEAD_W), 1)
    qq = jnp.concatenate([q, q], axis=0)
    qbd = jnp.where((row < t) == (col < HEAD_DIM), qq, jnp.zeros_like(qq))
    nt = (((1,), (1,)), ((), ()))
    s_c = lax.dot_general(qbd, ck_ref[0].astype(BF16), nt, preferred_element_type=F32)
    s_n = lax.dot_general(qbd, kn_ref[...].astype(BF16), nt, preferred_element_type=F32)
    m = jnp.maximum(jnp.max(s_c, axis=-1, keepdims=True), jnp.max(s_n, axis=-1, keepdims=True))
    p_c = jnp.exp(s_c - m)
    p_n = jnp.exp(s_n - m)
    l = jnp.sum(p_c, axis=-1, keepdims=True) + jnp.sum(p_n, axis=-1, keepdims=True)
    a = (_dot(p_c.astype(BF16), cv_ref[0].astype(BF16))
         + _dot(p_n.astype(BF16), vn_ref[...].astype(BF16))) / l
    lam = _lambda(lq1_ref, lk1_ref, lq2_ref, lk2_ref, lam_init)
    o = a[:t] - lam * a[t:]
    o = _rms(o, sg_ref[...]) * (1.0 - lam_init)
    o_ref[...] = o.astype(BF16)


def _attn_sample(q, cache_k, cache_v, k_new, v_new, lams, sg_row, lam_init, *, t):
    n_streams, past = cache_k.shape[0], cache_k.shape[1]
    new_spec = pl.BlockSpec((t, HEAD_W), lambda b, h: (b, h))
    cache_spec = pl.BlockSpec((1, past, HEAD_W), lambda b, h: (b, 0, h))
    lam_spec = pl.BlockSpec((1, HEAD_DIM), lambda b, h: (0, 0))
    return pl.pallas_call(
        functools.partial(_attn_sample_kernel, t=t, lam_init=lam_init),
        grid=(n_streams, N_HEADS),
        in_specs=[new_spec, cache_spec, cache_spec, new_spec, new_spec,
                  lam_spec, lam_spec, lam_spec, lam_spec,
                  pl.BlockSpec((1, HEAD_W), lambda b, h: (0, 0))],
        out_specs=new_spec,
        out_shape=jax.ShapeDtypeStruct((n_streams * t, N_HEADS * HEAD_W), BF16),
        compiler_params=_params(2), name="attn_sample",
    )(q, cache_k, cache_v, k_new, v_new, *lams, sg_row)


def kernel(x_prompt, x_sample, cache_k, cache_v, state_conv, ffn_norm, ffn_w_gate, ffn_w_up, ffn_w_down, conv_norm, conv_w_pw1, conv_b_pw1, conv_w_dw, conv_b_dw, conv_ln_g, conv_ln_b, conv_w_pw2, conv_b_pw2, kv_norm, w_k, w_v, k_norm, attn_norm, w_q, q_norm, lambda_q1, lambda_k1, lambda_q2, lambda_k2, subln, w_o):
    batch, seq, _ = x_prompt.shape
    dec_batch, dec_seq, _ = x_sample.shape
    assert batch == 1 and seq % ATTN_TILE == 0 and ATTN_TILE == ROW_TILE
    assert ffn_norm.shape[0] == 2 and conv_norm.shape[0] == 1 and attn_norm.shape[0] == 1
    past = cache_k.shape[1]
    qk = N_HEADS * HEAD_W
    row = lambda v: v.reshape(1, -1).astype(F32)
    bf = lambda w: w.astype(BF16)

    ffn_w = [[(row(ffn_norm[l, i]), bf(ffn_w_gate[l, i]), bf(ffn_w_up[l, i]), bf(ffn_w_down[l, i]))
              for i in range(2)] for l in range(2)]
    conv_w = (row(conv_norm[0]), bf(conv_w_pw1[0]), row(conv_b_pw1[0]), conv_w_dw[0].astype(F32),
              row(conv_b_dw[0]), row(conv_ln_g[0]), row(conv_ln_b[0]), bf(conv_w_pw2[0]),
              row(conv_b_pw2[0]))
    kv_w = (row(kv_norm), bf(w_k), bf(w_v), row(jnp.tile(k_norm, qk // HEAD_DIM)))
    q_w = (row(attn_norm[0]), bf(w_q[0]), row(jnp.tile(q_norm[0], qk // HEAD_DIM)))
    lams = (row(lambda_q1[0]), row(lambda_k1[0]), row(lambda_q2[0]), row(lambda_k2[0]))
    lam_init = 0.8 - 0.6 * math.exp(-0.3 * 1)
    sg = subln[0].astype(F32)
    wo = bf(w_o[0])

    def layer0(x, hist, nb, t):
        x = _ffn(x, *ffn_w[0][0])
        x, new_hist = _conv(x, hist, *conv_w, nb=nb, t=t)
        x = _ffn(x, *ffn_w[0][1])
        return x, new_hist, _kv(x, *kv_w)

    xp = x_prompt.reshape(seq, D_MODEL)
    zero_hist = jnp.zeros((1, CONV_HIST, CONV_C), F32)
    xp, hist_p, (k_p, v_p, kb_p, vt_p) = layer0(xp, zero_hist, 1, ROW_TILE)
    xp = _ffn(xp, *ffn_w[1][0])
    a_p = _attn_prompt(_q(xp, *q_w), kb_p, vt_p, lams, sg.reshape(HEAD_W, 1), lam_init)
    yp = _ffn(xp, *ffn_w[1][1], attn=a_p, wo=wo)

    xs = x_sample.reshape(dec_batch * dec_seq, D_MODEL)
    nb_s = 8
    xs, hist_s, (k_s, v_s, _, _) = layer0(xs, state_conv[0], nb_s, dec_seq)
    xs = _ffn(xs, *ffn_w[1][0])
    a_s = _attn_sample(_q(xs, *q_w), cache_k.reshape(dec_batch, past, qk),
                       cache_v.reshape(dec_batch, past, qk), k_s, v_s, lams,
                       sg.reshape(1, HEAD_W), lam_init, t=dec_seq)
    ys = _ffn(xs, *ffn_w[1][1], attn=a_s, wo=wo)

    return (yp.reshape(1, seq, D_MODEL),
            ys.reshape(dec_batch, dec_seq, D_MODEL),
            k_p.reshape(1, seq, N_HEADS, 2, HEAD_DIM),
            v_p.reshape(1, seq, N_HEADS, HEAD_W),
            hist_p.reshape(1, 1, CONV_HIST, CONV_C),
            k_s.reshape(dec_batch, dec_seq, N_HEADS, 2, HEAD_DIM),
            v_s.reshape(dec_batch, dec_seq, N_HEADS, HEAD_W),
            hist_s.reshape(1, dec_batch, CONV_HIST, CONV_C))
```

```python
import functools
import math

import jax
import jax.numpy as jnp
from jax import lax
from jax.experimental import pallas as pl
from jax.experimental.pallas import tpu as pltpu

D_MODEL = 1024
D_FF = 2816
CONV_C = 1024
CONV_W = 31
CONV_HIST = CONV_W - 1
N_HEADS = 8
HEAD_DIM = 64
HEAD_W = 2 * HEAD_DIM
CHUNK = 64
EPS = 1e-6
SCALE = 1.0 / math.sqrt(HEAD_DIM)
LOG2E = math.log2(math.e)
NEG = -1e30

F32 = jnp.float32
BF16 = jnp.bfloat16

SUBLANES = 8
MXU_DIM = 256
ROW_TILE = 512
ATTN_TILE = 512
CONV_PAD = 32
VMEM_LIMIT_BYTES = 56 * 1024 * 1024


def _params(n_grid_dims):
    return pltpu.CompilerParams(
        dimension_semantics=("arbitrary",) * n_grid_dims,
        vmem_limit_bytes=VMEM_LIMIT_BYTES)


def _const_spec(shape):
    zeros = (0,) * len(shape)
    return pl.BlockSpec(shape, lambda *_: zeros, pipeline_mode=pl.Buffered(1))


def _rows_spec(tm, width):
    return pl.BlockSpec((tm, width), lambda i: (i, 0))


def _dot(a, b):
    return jnp.dot(a, b, preferred_element_type=F32)


def _rms(x, g):
    return x * lax.rsqrt(jnp.mean(x * x, axis=-1, keepdims=True) + EPS) * g


def _group_rms(y, g_tiled):
    r = lax.broadcasted_iota(jnp.int32, (MXU_DIM, MXU_DIM), 0) // HEAD_DIM
    c = lax.broadcasted_iota(jnp.int32, (MXU_DIM, MXU_DIM), 1) // HEAD_DIM
    ones_bd = (r == c).astype(BF16)
    outs = []
    for j in range(y.shape[1] // MXU_DIM):
        ys = y[:, j * MXU_DIM:(j + 1) * MXU_DIM]
        ss = _dot((ys * ys).astype(BF16), ones_bd)
        outs.append(ys * lax.rsqrt(ss * (1.0 / HEAD_DIM) + EPS))
    return jnp.concatenate(outs, axis=1) * g_tiled


def _ffn_body(x, g_ref, wg_ref, wu_ref, wd_ref):
    h = _rms(x, g_ref[...]).astype(BF16)
    gate = _dot(h, wg_ref[...])
    up = _dot(h, wu_ref[...])
    act = (gate * jax.nn.sigmoid(gate) * up).astype(BF16)
    return x + 0.5 * _dot(act, wd_ref[...])


def _ffn_kernel(x_ref, g_ref, wg_ref, wu_ref, wd_ref, o_ref):
    o_ref[...] = _ffn_body(x_ref[...], g_ref, wg_ref, wu_ref, wd_ref)


def _proj_ffn_kernel(x_ref, a_ref, wo_ref, g_ref, wg_ref, wu_ref, wd_ref, o_ref):
    x = x_ref[...] + _dot(a_ref[...], wo_ref[...])
    o_ref[...] = _ffn_body(x, g_ref, wg_ref, wu_ref, wd_ref)


def _ffn(x, g, wg, wu, wd, attn=None, wo=None):
    rows = x.shape[0]
    tm = min(ROW_TILE, rows)
    w_specs = [_const_spec((1, D_MODEL)), _const_spec((D_MODEL, D_FF)),
               _const_spec((D_MODEL, D_FF)), _const_spec((D_FF, D_MODEL))]
    if attn is None:
        body, ins = _ffn_kernel, (x, g, wg, wu, wd)
        in_specs = [_rows_spec(tm, D_MODEL)] + w_specs
    else:
        body, ins = _proj_ffn_kernel, (x, attn, wo, g, wg, wu, wd)
        in_specs = [_rows_spec(tm, D_MODEL), _rows_spec(tm, D_MODEL),
                    _const_spec((D_MODEL, D_MODEL))] + w_specs
    return pl.pallas_call(
        body, grid=(rows // tm,), in_specs=in_specs,
        out_specs=_rows_spec(tm, D_MODEL),
        out_shape=jax.ShapeDtypeStruct((rows, D_MODEL), F32),
        compiler_params=_params(1), name="ffn" if attn is None else "proj_ffn",
    )(*ins)


def _conv_kernel(x_ref, hist_ref, g_ref, w1_ref, b1_ref, wdw_ref, bdw_ref,
                 lng_ref, lnb_ref, w2_ref, b2_ref, o_ref, hist_out_ref,
                 shift_ref, c_ref, *, nb, t, carry):
    i = pl.program_id(0)
    length = CONV_PAD + t
    x = x_ref[...]
    h = _rms(x, g_ref[...]).astype(BF16)
    u = _dot(h, w1_ref[...]) + b1_ref[...]
    glu = u[:, :CONV_C] * jax.nn.sigmoid(u[:, CONV_C:])

    lead = CONV_PAD - CONV_HIST
    if carry:
        @pl.when(i == 0)
        def _():
            shift_ref[0, :, lead:CONV_PAD, :] = hist_ref[...]
    else:
        shift_ref[0, :, lead:CONV_PAD, :] = hist_ref[...]
    shift_ref[0, :, 0:lead, :] = jnp.zeros((nb, lead, CONV_C), F32)
    shift_ref[0, :, CONV_PAD:, :] = glu.reshape(nb, t, CONV_C)
    for s in range(1, SUBLANES):
        shift_ref[s, :, 0:length - SUBLANES, :] = shift_ref[0, :, s:s + length - SUBLANES, :]

    def conv_rows(b, j):
        r0 = pl.multiple_of(j * SUBLANES, SUBLANES)
        acc = jnp.broadcast_to(bdw_ref[...], (SUBLANES, CONV_C))
        for w in range(CONV_W):
            off = lead + w
            s = off % SUBLANES
            tap = shift_ref[s, b, pl.ds(r0 + (off - s), SUBLANES), :]
            acc = acc + tap * wdw_ref[w:w + 1, :]
        c_ref[pl.ds(pl.multiple_of(b * t + r0, SUBLANES), SUBLANES), :] = acc

    def conv_stream(b, carry_val):
        def conv_step(j, c):
            conv_rows(b, j)
            return c
        return lax.fori_loop(0, t // SUBLANES, conv_step, carry_val)

    lax.fori_loop(0, nb, conv_stream, 0)

    c = c_ref[...]
    mu = jnp.mean(c, axis=-1, keepdims=True)
    cc = c - mu
    var = jnp.mean(cc * cc, axis=-1, keepdims=True)
    y = cc * lax.rsqrt(var + EPS) * lng_ref[...] + lnb_ref[...]
    y = (y * jax.nn.sigmoid(y)).astype(BF16)
    o_ref[...] = x + _dot(y, w2_ref[...]) + b2_ref[...]

    tail = shift_ref[0, :, length - CONV_HIST:length, :]
    hist_out_ref[...] = tail
    if carry:
        shift_ref[0, :, lead:CONV_PAD, :] = tail


def _conv(x, hist, g, w1, b1, wdw, bdw, lng, lnb, w2, b2, *, nb, t):
    rows = x.shape[0]
    n_streams = hist.shape[0]
    tm = nb * t
    carry = nb == 1 and rows // n_streams > t
    if carry:
        assert n_streams == 1
        hist_map = lambda i: (0, 0, 0)
    else:
        assert rows == n_streams * t
        hist_map = lambda i: (i, 0, 0)
    vec = lambda n: _const_spec((1, n))
    return pl.pallas_call(
        functools.partial(_conv_kernel, nb=nb, t=t, carry=carry),
        grid=(rows // tm,),
        in_specs=[_rows_spec(tm, D_MODEL),
                  pl.BlockSpec((nb, CONV_HIST, CONV_C), hist_map),
                  vec(D_MODEL), _const_spec((D_MODEL, 2 * CONV_C)), vec(2 * CONV_C),
                  _const_spec((CONV_W, CONV_C)), vec(CONV_C), vec(CONV_C), vec(CONV_C),
                  _const_spec((CONV_C, D_MODEL)), vec(D_MODEL)],
        out_specs=[_rows_spec(tm, D_MODEL),
                   pl.BlockSpec((nb, CONV_HIST, CONV_C), hist_map)],
        out_shape=[jax.ShapeDtypeStruct((rows, D_MODEL), F32),
                   jax.ShapeDtypeStruct((n_streams, CONV_HIST, CONV_C), F32)],
        scratch_shapes=[pltpu.VMEM((SUBLANES, nb, CONV_PAD + t, CONV_C), F32),
                        pltpu.VMEM((tm, CONV_C), F32)],
        compiler_params=_params(1), name="conv_module",
    )(x, hist, g, w1, b1, wdw, bdw, lng, lnb, w2, b2)


def _kv_kernel(x_ref, g_ref, wk_ref, wv_ref, kg_ref, k_ref, v_ref, kb_ref, vt_ref):
    h = _rms(x_ref[...], g_ref[...]).astype(BF16)
    k = _group_rms(_dot(h, wk_ref[...]), kg_ref[...])
    v = _dot(h, wv_ref[...])
    k_ref[...] = k
    v_ref[...] = v
    kb_ref[...] = k.astype(BF16)
    tm = v.shape[0]
    vt_ref[:, 0] = v.T.astype(BF16).reshape(N_HEADS, HEAD_W, tm)


def _kv(x, g, wk, wv, kg):
    rows = x.shape[0]
    tm = min(ROW_TILE, rows)
    n = rows // tm
    qk = N_HEADS * HEAD_W
    return pl.pallas_call(
        _kv_kernel, grid=(n,),
        in_specs=[_rows_spec(tm, D_MODEL), _const_spec((1, D_MODEL)),
                  _const_spec((D_MODEL, qk)), _const_spec((D_MODEL, qk)),
                  _const_spec((1, qk))],
        out_specs=[_rows_spec(tm, qk), _rows_spec(tm, qk), _rows_spec(tm, qk),
                   pl.BlockSpec((N_HEADS, 1, HEAD_W, tm), lambda i: (0, i, 0, 0))],
        out_shape=[jax.ShapeDtypeStruct((rows, qk), F32),
                   jax.ShapeDtypeStruct((rows, qk), F32),
                   jax.ShapeDtypeStruct((rows, qk), BF16),
                   jax.ShapeDtypeStruct((N_HEADS, n, HEAD_W, tm), BF16)],
        compiler_params=_params(1), name="kv_proj",
    )(x, g, wk, wv, kg)


def _q_kernel(x_ref, g_ref, wq_ref, qg_ref, q_ref):
    h = _rms(x_ref[...], g_ref[...]).astype(BF16)
    q = _group_rms(_dot(h, wq_ref[...]), qg_ref[...]) * (SCALE * LOG2E)
    q_ref[...] = q.astype(BF16)


def _q(x, g, wq, qg):
    rows = x.shape[0]
    tm = min(ROW_TILE, rows)
    qk = N_HEADS * HEAD_W
    return pl.pallas_call(
        _q_kernel, grid=(rows // tm,),
        in_specs=[_rows_spec(tm, D_MODEL), _const_spec((1, D_MODEL)),
                  _const_spec((D_MODEL, qk)), _const_spec((1, qk))],
        out_specs=_rows_spec(tm, qk),
        out_shape=jax.ShapeDtypeStruct((rows, qk), BF16),
        compiler_params=_params(1), name="q_proj",
    )(x, g, wq, qg)


def _lambda(lq1_ref, lk1_ref, lq2_ref, lk2_ref, lam_init):
    s1 = jnp.sum(lq1_ref[...] * lk1_ref[...], axis=-1, keepdims=True)
    s2 = jnp.sum(lq2_ref[...] * lk2_ref[...], axis=-1, keepdims=True)
    return jnp.exp(s1) - jnp.exp(s2) + lam_init


def _attn_prompt_kernel(q_ref, k_ref, vt_ref, lq1_ref, lk1_ref, lq2_ref, lk2_ref,
                        sg_ref, o_ref, qbd_ref, s0_ref, s1_ref, mb0_ref, mb1_ref,
                        m_ref, l_ref, acc_ref, *, tile, lam_init):
    qi = pl.program_id(1)
    q_t = q_ref[...].astype(F32).T
    row = lax.broadcasted_iota(jnp.int32, (HEAD_W, 2 * tile), 0)
    col = lax.broadcasted_iota(jnp.int32, (HEAD_W, 2 * tile), 1)
    qq = jnp.concatenate([q_t, q_t], axis=1)
    qbd_ref[...] = jnp.where((row < HEAD_DIM) == (col < tile), qq, 0.0).astype(BF16)
    m_ref[...] = jnp.full(m_ref.shape, NEG, F32)
    l_ref[...] = jnp.zeros(l_ref.shape, F32)
    acc_ref[...] = jnp.zeros(acc_ref.shape, F32)
    bufs = ((s0_ref, mb0_ref), (s1_ref, mb1_ref))

    def scores(j, buf, diagonal=False):
        s_ref, mb_ref = buf
        kb = k_ref[pl.ds(pl.multiple_of(j * tile, tile), tile), :]
        s = _dot(kb, qbd_ref[...])
        if diagonal:
            key_chunk = lax.broadcasted_iota(jnp.int32, s.shape, 0) // CHUNK
            qry = lax.broadcasted_iota(jnp.int32, s.shape, 1)
            qry_chunk = jnp.where(qry >= tile, qry - tile, qry) // CHUNK
            s = jnp.where(key_chunk <= qry_chunk, s, NEG)
        s_ref[...] = s
        mb_ref[...] = jnp.max(s, axis=0, keepdims=True)

    def update(j, buf):
        s_ref, mb_ref = buf
        vt = vt_ref[0, j]
        for c in range(2 * tile // MXU_DIM):
            cols = slice(c * MXU_DIM, (c + 1) * MXU_DIM)
            m_old = m_ref[:, cols]
            m_new = jnp.maximum(m_old, mb_ref[:, cols])
            alpha = jnp.exp2(m_old - m_new)
            p = jnp.exp2(s_ref[:, cols] - m_new)
            l_ref[:, cols] = alpha * l_ref[:, cols] + jnp.sum(p, axis=0, keepdims=True)
            acc_ref[:, cols] = acc_ref[:, cols] * alpha + _dot(vt, p.astype(BF16))
            m_ref[:, cols] = m_new

    odd = qi % 2

    @pl.when(odd == 0)
    def _():
        scores(qi, bufs[0], diagonal=True)

    @pl.when(odd == 1)
    def _():
        scores(qi, bufs[1], diagonal=True)
        scores(0, bufs[0])
        update(qi, bufs[1])

    def two_steps(i, pending):
        j = odd + 2 * i
        scores(j, bufs[1])
        update(pending, bufs[0])
        scores(j + 1, bufs[0])
        update(j, bufs[1])
        return j + 1

    pending = lax.fori_loop(0, qi // 2, two_steps, jnp.where(odd == 1, 0, qi))
    update(pending, bufs[0])

    a = acc_ref[...] / l_ref[...]
    lam = _lambda(lq1_ref, lk1_ref, lq2_ref, lk2_ref, lam_init)
    o_t = a[:, :tile] - lam * a[:, tile:]
    ms = jnp.mean(o_t * o_t, axis=0, keepdims=True)
    o_t = o_t * lax.rsqrt(ms + EPS) * sg_ref[...] * (1.0 - lam_init)
    o_ref[...] = o_t.T.astype(BF16)


def _attn_prompt(q, kb, vt, lams, sg_col, lam_init):
    seq = q.shape[0]
    tile = ATTN_TILE
    n = seq // tile
    lam_spec = pl.BlockSpec((1, HEAD_DIM), lambda h, i: (0, 0))
    return pl.pallas_call(
        functools.partial(_attn_prompt_kernel, tile=tile, lam_init=lam_init),
        grid=(N_HEADS, n),
        in_specs=[pl.BlockSpec((tile, HEAD_W), lambda h, i: (i, h)),
                  pl.BlockSpec((seq, HEAD_W), lambda h, i: (0, h)),
                  pl.BlockSpec((1, n, HEAD_W, tile), lambda h, i: (h, 0, 0, 0)),
                  lam_spec, lam_spec, lam_spec, lam_spec,
                  pl.BlockSpec((HEAD_W, 1), lambda h, i: (0, 0))],
        out_specs=pl.BlockSpec((tile, HEAD_W), lambda h, i: (i, h)),
        out_shape=jax.ShapeDtypeStruct((seq, N_HEADS * HEAD_W), BF16),
        scratch_shapes=[pltpu.VMEM((HEAD_W, 2 * tile), BF16),
                        pltpu.VMEM((tile, 2 * tile), F32),
                        pltpu.VMEM((tile, 2 * tile), F32),
                        pltpu.VMEM((1, 2 * tile), F32),
                        pltpu.VMEM((1, 2 * tile), F32),
                        pltpu.VMEM((1, 2 * tile), F32),
                        pltpu.VMEM((1, 2 * tile), F32),
                        pltpu.VMEM((HEAD_W, 2 * tile), F32)],
        compiler_params=_params(2), name="attn_prompt",
    )(q, kb, vt, *lams, sg_col)


def _attn_sample_kernel(q_ref, ck_ref, cv_ref, kn_ref, vn_ref, lq1_ref, lk1_ref,
                        lq2_ref, lk2_ref, sg_ref, o_ref, *, t, lam_init):
    q = q_ref[...]
    row = lax.broadcasted_iota(jnp.int32, (2 * t, HEAD_W), 0)
    col = lax.broadcasted_iota(jnp.int32, (2 * t, HEAD_W), 1)
    qq = jnp.concatenate([q, q], axis=0)
    qbd = jnp.where((row < t) == (col < HEAD_DIM), qq, jnp.zeros_like(qq))
    nt = (((1,), (1,)), ((), ()))
    s_c = lax.dot_general(qbd, ck_ref[0].astype(BF16), nt, preferred_element_type=F32)
    s_n = lax.dot_general(qbd, kn_ref[...].astype(BF16), nt, preferred_element_type=F32)
    m = jnp.maximum(jnp.max(s_c, axis=-1, keepdims=True), jnp.max(s_n, axis=-1, keepdims=True))
    p_c = jnp.exp2(s_c - m)
    p_n = jnp.exp2(s_n - m)
    l = jnp.sum(p_c, axis=-1, keepdims=True) + jnp.sum(p_n, axis=-1, keepdims=True)
    a = (_dot(p_c.astype(BF16), cv_ref[0].astype(BF16))
         + _dot(p_n.astype(BF16), vn_ref[...].astype(BF16))) / l
    lam = _lambda(lq1_ref, lk1_ref, lq2_ref, lk2_ref, lam_init)
    o = a[:t] - lam * a[t:]
    o = _rms(o, sg_ref[...]) * (1.0 - lam_init)
    o_ref[...] = o.astype(BF16)


def _attn_sample(q, cache_k, cache_v, k_new, v_new, lams, sg_row, lam_init, *, t):
    n_streams, past = cache_k.shape[0], cache_k.shape[1]
    new_spec = pl.BlockSpec((t, HEAD_W), lambda b, h: (b, h))
    cache_spec = pl.BlockSpec((1, past, HEAD_W), lambda b, h: (b, 0, h))
    lam_spec = pl.BlockSpec((1, HEAD_DIM), lambda b, h: (0, 0))
    return pl.pallas_call(
        functools.partial(_attn_sample_kernel, t=t, lam_init=lam_init),
        grid=(n_streams, N_HEADS),
        in_specs=[new_spec, cache_spec, cache_spec, new_spec, new_spec,
                  lam_spec, lam_spec, lam_spec, lam_spec,
                  pl.BlockSpec((1, HEAD_W), lambda b, h: (0, 0))],
        out_specs=new_spec,
        out_shape=jax.ShapeDtypeStruct((n_streams * t, N_HEADS * HEAD_W), BF16),
        compiler_params=_params(2), name="attn_sample",
    )(q, cache_k, cache_v, k_new, v_new, *lams, sg_row)


def kernel(x_prompt, x_sample, cache_k, cache_v, state_conv, ffn_norm, ffn_w_gate, ffn_w_up, ffn_w_down, conv_norm, conv_w_pw1, conv_b_pw1, conv_w_dw, conv_b_dw, conv_ln_g, conv_ln_b, conv_w_pw2, conv_b_pw2, kv_norm, w_k, w_v, k_norm, attn_norm, w_q, q_norm, lambda_q1, lambda_k1, lambda_q2, lambda_k2, subln, w_o):
    batch, seq, _ = x_prompt.shape
    dec_batch, dec_seq, _ = x_sample.shape
    assert batch == 1 and seq % ATTN_TILE == 0 and ATTN_TILE == ROW_TILE
    assert ffn_norm.shape[0] == 2 and conv_norm.shape[0] == 1 and attn_norm.shape[0] == 1
    past = cache_k.shape[1]
    qk = N_HEADS * HEAD_W
    row = lambda v: v.reshape(1, -1).astype(F32)
    bf = lambda w: w.astype(BF16)

    ffn_w = [[(row(ffn_norm[l, i]), bf(ffn_w_gate[l, i]), bf(ffn_w_up[l, i]), bf(ffn_w_down[l, i]))
              for i in range(2)] for l in range(2)]
    conv_w = (row(conv_norm[0]), bf(conv_w_pw1[0]), row(conv_b_pw1[0]), conv_w_dw[0].astype(F32),
              row(conv_b_dw[0]), row(conv_ln_g[0]), row(conv_ln_b[0]), bf(conv_w_pw2[0]),
              row(conv_b_pw2[0]))
    kv_w = (row(kv_norm), bf(w_k), bf(w_v), row(jnp.tile(k_norm, qk // HEAD_DIM)))
    q_w = (row(attn_norm[0]), bf(w_q[0]), row(jnp.tile(q_norm[0], qk // HEAD_DIM)))
    lams = (row(lambda_q1[0]), row(lambda_k1[0]), row(lambda_q2[0]), row(lambda_k2[0]))
    lam_init = 0.8 - 0.6 * math.exp(-0.3 * 1)
    sg = subln[0].astype(F32)
    wo = bf(w_o[0])

    def layer0(x, hist, nb, t):
        x = _ffn(x, *ffn_w[0][0])
        x, new_hist = _conv(x, hist, *conv_w, nb=nb, t=t)
        x = _ffn(x, *ffn_w[0][1])
        return x, new_hist, _kv(x, *kv_w)

    xp = x_prompt.reshape(seq, D_MODEL)
    zero_hist = jnp.zeros((1, CONV_HIST, CONV_C), F32)
    xp, hist_p, (k_p, v_p, kb_p, vt_p) = layer0(xp, zero_hist, 1, ROW_TILE)
    xp = _ffn(xp, *ffn_w[1][0])
    a_p = _attn_prompt(_q(xp, *q_w), kb_p, vt_p, lams, sg.reshape(HEAD_W, 1), lam_init)
    yp = _ffn(xp, *ffn_w[1][1], attn=a_p, wo=wo)

    xs = x_sample.reshape(dec_batch * dec_seq, D_MODEL)
    nb_s = 8
    xs, hist_s, (k_s, v_s, _, _) = layer0(xs, state_conv[0], nb_s, dec_seq)
    xs = _ffn(xs, *ffn_w[1][0])
    a_s = _attn_sample(_q(xs, *q_w), cache_k.reshape(dec_batch, past, qk),
                       cache_v.reshape(dec_batch, past, qk), k_s, v_s, lams,
                       sg.reshape(1, HEAD_W), lam_init, t=dec_seq)
    ys = _ffn(xs, *ffn_w[1][1], attn=a_s, wo=wo)

    return (yp.reshape(1, seq, D_MODEL),
            ys.reshape(dec_batch, dec_seq, D_MODEL),
            k_p.reshape(1, seq, N_HEADS, 2, HEAD_DIM),
            v_p.reshape(1, seq, N_HEADS, HEAD_W),
            hist_p.reshape(1, 1, CONV_HIST, CONV_C),
            k_s.reshape(dec_batch, dec_seq, N_HEADS, 2, HEAD_DIM),
            v_s.reshape(dec_batch, dec_seq, N_HEADS, HEAD_W),
            hist_s.reshape(1, dec_batch, CONV_HIST, CONV_C))
```

```python
import functools
import math

import jax
import jax.numpy as jnp
from jax import lax
from jax.experimental import pallas as pl
from jax.experimental.pallas import tpu as pltpu

D_MODEL = 1024
D_FF = 2816
CONV_C = 1024
CONV_W = 31
CONV_HIST = CONV_W - 1
N_HEADS = 8
HEAD_DIM = 64
HEAD_W = 2 * HEAD_DIM
VT_ROWS = HEAD_W + 16
CHUNK = 64
EPS = 1e-6
SCALE = 1.0 / math.sqrt(HEAD_DIM)
LOG2E = math.log2(math.e)
NEG = -1e30

F32 = jnp.float32
BF16 = jnp.bfloat16

SUBLANES = 8
LANES = 128
MXU_DIM = 256
ROW_TILE = 512
ATTN_K_TILE = ROW_TILE
ATTN_Q_TILE = 2 * ATTN_K_TILE
CONV_PAD = 32
CONV_ROWS = 32
VMEM_LIMIT_BYTES = 56 * 1024 * 1024


def _params(n_grid_dims):
    return pltpu.CompilerParams(
        dimension_semantics=("arbitrary",) * n_grid_dims,
        vmem_limit_bytes=VMEM_LIMIT_BYTES)


def _const_spec(shape):
    zeros = (0,) * len(shape)
    return pl.BlockSpec(shape, lambda *_: zeros, pipeline_mode=pl.Buffered(1))


def _rows_spec(tm, width):
    return pl.BlockSpec((tm, width), lambda i: (i, 0))


def _dot(a, b):
    return jnp.dot(a, b, preferred_element_type=F32)


def _rms(x, g):
    return x * lax.rsqrt(jnp.mean(x * x, axis=-1, keepdims=True) + EPS) * g


def _group_rms(y, g_tiled):
    r = lax.broadcasted_iota(jnp.int32, (MXU_DIM, MXU_DIM), 0) // HEAD_DIM
    c = lax.broadcasted_iota(jnp.int32, (MXU_DIM, MXU_DIM), 1) // HEAD_DIM
    ones_bd = (r == c).astype(BF16)
    outs = []
    for j in range(y.shape[1] // MXU_DIM):
        ys = y[:, j * MXU_DIM:(j + 1) * MXU_DIM]
        ss = _dot((ys * ys).astype(BF16), ones_bd)
        outs.append(ys * lax.rsqrt(ss * (1.0 / HEAD_DIM) + EPS))
    return jnp.concatenate(outs, axis=1) * g_tiled


def _ffn_body(x, g_ref, wg_ref, wu_ref, wd_ref):
    h = _rms(x, g_ref[...]).astype(BF16)
    gate = _dot(h, wg_ref[...])
    up = _dot(h, wu_ref[...])
    act = (gate * jax.nn.sigmoid(gate) * up).astype(BF16)
    return x + 0.5 * _dot(act, wd_ref[...])


def _ffn_kernel(x_ref, g_ref, wg_ref, wu_ref, wd_ref, o_ref):
    o_ref[...] = _ffn_body(x_ref[...], g_ref, wg_ref, wu_ref, wd_ref)


def _proj_ffn_kernel(x_ref, a_ref, wo_ref, g_ref, wg_ref, wu_ref, wd_ref, o_ref):
    x = x_ref[...] + _dot(a_ref[...], wo_ref[...])
    o_ref[...] = _ffn_body(x, g_ref, wg_ref, wu_ref, wd_ref)


def _ffn(x, g, wg, wu, wd, attn=None, wo=None):
    rows = x.shape[0]
    tm = min(ROW_TILE, rows)
    w_specs = [_const_spec((1, D_MODEL)), _const_spec((D_MODEL, D_FF)),
               _const_spec((D_MODEL, D_FF)), _const_spec((D_FF, D_MODEL))]
    if attn is None:
        body, ins = _ffn_kernel, (x, g, wg, wu, wd)
        in_specs = [_rows_spec(tm, D_MODEL)] + w_specs
    else:
        body, ins = _proj_ffn_kernel, (x, attn, wo, g, wg, wu, wd)
        in_specs = [_rows_spec(tm, D_MODEL), _rows_spec(tm, D_MODEL),
                    _const_spec((D_MODEL, D_MODEL))] + w_specs
    return pl.pallas_call(
        body, grid=(rows // tm,), in_specs=in_specs,
        out_specs=_rows_spec(tm, D_MODEL),
        out_shape=jax.ShapeDtypeStruct((rows, D_MODEL), F32),
        compiler_params=_params(1), name="ffn" if attn is None else "proj_ffn",
    )(*ins)


def _conv_kernel(x_ref, hist_ref, g_ref, w1_ref, b1_ref, wdw_ref, bdw_ref,
                 lng_ref, lnb_ref, w2_ref, b2_ref, o_ref, hist_out_ref,
                 shift_ref, c_ref, *, nb, t, carry):
    i = pl.program_id(0)
    length = CONV_PAD + t
    x = x_ref[...]
    h = _rms(x, g_ref[...]).astype(BF16)
    u = _dot(h, w1_ref[...]) + b1_ref[...]
    glu = u[:, :CONV_C] * jax.nn.sigmoid(u[:, CONV_C:])

    lead = CONV_PAD - CONV_HIST
    if carry:
        @pl.when(i == 0)
        def _():
            shift_ref[0, :, lead:CONV_PAD, :] = hist_ref[...]
    else:
        shift_ref[0, :, lead:CONV_PAD, :] = hist_ref[...]
    shift_ref[0, :, 0:lead, :] = jnp.zeros((nb, lead, CONV_C), F32)
    shift_ref[0, :, CONV_PAD:, :] = glu.reshape(nb, t, CONV_C)
    for s in range(1, SUBLANES):
        shift_ref[s, :, 0:length - SUBLANES, :] = shift_ref[0, :, s:s + length - SUBLANES, :]

    for lt in range(CONV_C // LANES):
        lanes = slice(lt * LANES, (lt + 1) * LANES)
        tap_w = [wdw_ref[w, :, lanes] for w in range(CONV_W)]
        bias = bdw_ref[:, lanes]

        def conv_rows(b, j, lanes=lanes, tap_w=tap_w, bias=bias):
            base = pl.multiple_of(j * CONV_ROWS, CONV_ROWS)
            frames = {}

            def window(off):
                if off not in frames:
                    s = off % SUBLANES
                    frames[off] = shift_ref[s, b, pl.ds(base + (off - s), SUBLANES), lanes]
                return frames[off]

            for k in range(CONV_ROWS // SUBLANES):
                acc = [bias, None]
                for w in range(CONV_W):
                    term = window(k * SUBLANES + lead + w) * tap_w[w]
                    acc[w % 2] = term if acc[w % 2] is None else acc[w % 2] + term
                rows = pl.ds(pl.multiple_of(b * t + base + k * SUBLANES, SUBLANES), SUBLANES)
                c_ref[rows, lanes] = acc[0] + acc[1]

        def conv_stream(b, carry_val, conv_rows=conv_rows):
            def conv_step(j, c):
                conv_rows(b, j)
                return c
            return lax.fori_loop(0, t // CONV_ROWS, conv_step, carry_val)

        lax.fori_loop(0, nb, conv_stream, 0)

    c = c_ref[...]
    mu = jnp.mean(c, axis=-1, keepdims=True)
    cc = c - mu
    var = jnp.mean(cc * cc, axis=-1, keepdims=True)
    y = cc * lax.rsqrt(var + EPS) * lng_ref[...] + lnb_ref[...]
    y = (y * jax.nn.sigmoid(y)).astype(BF16)
    o_ref[...] = x + _dot(y, w2_ref[...]) + b2_ref[...]

    tail = shift_ref[0, :, length - CONV_HIST:length, :]
    hist_out_ref[...] = tail
    if carry:
        shift_ref[0, :, lead:CONV_PAD, :] = tail


def _conv(x, hist, g, w1, b1, wdw, bdw, lng, lnb, w2, b2, *, nb, t):
    rows = x.shape[0]
    n_streams = hist.shape[0]
    tm = nb * t
    carry = nb == 1 and rows // n_streams > t
    if carry:
        assert n_streams == 1
        hist_map = lambda i: (0, 0, 0)
    else:
        assert rows == n_streams * t
        hist_map = lambda i: (i, 0, 0)
    vec = lambda n: _const_spec((1, n))
    return pl.pallas_call(
        functools.partial(_conv_kernel, nb=nb, t=t, carry=carry),
        grid=(rows // tm,),
        in_specs=[_rows_spec(tm, D_MODEL),
                  pl.BlockSpec((nb, CONV_HIST, CONV_C), hist_map),
                  vec(D_MODEL), _const_spec((D_MODEL, 2 * CONV_C)), vec(2 * CONV_C),
                  _const_spec((CONV_W, SUBLANES, CONV_C)), _const_spec((SUBLANES, CONV_C)),
                  vec(CONV_C), vec(CONV_C),
                  _const_spec((CONV_C, D_MODEL)), vec(D_MODEL)],
        out_specs=[_rows_spec(tm, D_MODEL),
                   pl.BlockSpec((nb, CONV_HIST, CONV_C), hist_map)],
        out_shape=[jax.ShapeDtypeStruct((rows, D_MODEL), F32),
                   jax.ShapeDtypeStruct((n_streams, CONV_HIST, CONV_C), F32)],
        scratch_shapes=[pltpu.VMEM((SUBLANES, nb, CONV_PAD + t, CONV_C), F32),
                        pltpu.VMEM((tm, CONV_C), F32)],
        compiler_params=_params(1), name="conv_module",
    )(x, hist, g, w1, b1, wdw, bdw, lng, lnb, w2, b2)


def _kv_kernel(x_ref, g_ref, wk_ref, wv_ref, kg_ref, k_ref, v_ref, kb_ref, vt_ref):
    h = _rms(x_ref[...], g_ref[...]).astype(BF16)
    k = _group_rms(_dot(h, wk_ref[...]), kg_ref[...])
    v = _dot(h, wv_ref[...])
    k_ref[...] = k
    v_ref[...] = v
    kb_ref[...] = k.astype(BF16)
    tm = v.shape[0]
    vt_ref[:, 0, :HEAD_W, :] = v.T.astype(BF16).reshape(N_HEADS, HEAD_W, tm)
    vt_ref[:, 0, HEAD_W:, :] = jnp.ones((N_HEADS, VT_ROWS - HEAD_W, tm), BF16)


def _kv(x, g, wk, wv, kg):
    rows = x.shape[0]
    tm = min(ROW_TILE, rows)
    n = rows // tm
    qk = N_HEADS * HEAD_W
    return pl.pallas_call(
        _kv_kernel, grid=(n,),
        in_specs=[_rows_spec(tm, D_MODEL), _const_spec((1, D_MODEL)),
                  _const_spec((D_MODEL, qk)), _const_spec((D_MODEL, qk)),
                  _const_spec((1, qk))],
        out_specs=[_rows_spec(tm, qk), _rows_spec(tm, qk), _rows_spec(tm, qk),
                   pl.BlockSpec((N_HEADS, 1, VT_ROWS, tm), lambda i: (0, i, 0, 0))],
        out_shape=[jax.ShapeDtypeStruct((rows, qk), F32),
                   jax.ShapeDtypeStruct((rows, qk), F32),
                   jax.ShapeDtypeStruct((rows, qk), BF16),
                   jax.ShapeDtypeStruct((N_HEADS, n, VT_ROWS, tm), BF16)],
        compiler_params=_params(1), name="kv_proj",
    )(x, g, wk, wv, kg)


def _q_kernel(x_ref, g_ref, wq_ref, qg_ref, q_ref):
    h = _rms(x_ref[...], g_ref[...]).astype(BF16)
    q = _group_rms(_dot(h, wq_ref[...]), qg_ref[...]) * (SCALE * LOG2E)
    q_ref[...] = q.astype(BF16)


def _q(x, g, wq, qg):
    rows = x.shape[0]
    tm = min(ROW_TILE, rows)
    qk = N_HEADS * HEAD_W
    return pl.pallas_call(
        _q_kernel, grid=(rows // tm,),
        in_specs=[_rows_spec(tm, D_MODEL), _const_spec((1, D_MODEL)),
                  _const_spec((D_MODEL, qk)), _const_spec((1, qk))],
        out_specs=_rows_spec(tm, qk),
        out_shape=jax.ShapeDtypeStruct((rows, qk), BF16),
        compiler_params=_params(1), name="q_proj",
    )(x, g, wq, qg)


def _lambda(lq1_ref, lk1_ref, lq2_ref, lk2_ref, lam_init):
    s1 = jnp.sum(lq1_ref[...] * lk1_ref[...], axis=-1, keepdims=True)
    s2 = jnp.sum(lq2_ref[...] * lk2_ref[...], axis=-1, keepdims=True)
    return jnp.exp(s1) - jnp.exp(s2) + lam_init


def _attn_prompt_kernel(q_ref, k_ref, vt_ref, lq1_ref, lk1_ref, lq2_ref, lk2_ref,
                        sg_ref, o_ref, qbd_ref, s0_ref, s1_ref, mb0_ref, mb1_ref,
                        m_ref, acc_ref, *, tq, tk, lam_init):
    qi = pl.program_id(1)
    q_t = q_ref[...].astype(F32).T
    row = lax.broadcasted_iota(jnp.int32, (HEAD_W, 2 * tq), 0)
    col = lax.broadcasted_iota(jnp.int32, (HEAD_W, 2 * tq), 1)
    qq = jnp.concatenate([q_t, q_t], axis=1)
    qbd_ref[...] = jnp.where((row < HEAD_DIM) == (col < tq), qq, 0.0).astype(BF16)
    m_ref[...] = jnp.full(m_ref.shape, NEG, F32)
    acc_ref[...] = jnp.zeros(acc_ref.shape, F32)
    bufs = ((s0_ref, mb0_ref), (s1_ref, mb1_ref))
    causal = (lax.broadcasted_iota(jnp.int32, (tk, 1), 0) // CHUNK
              <= lax.broadcasted_iota(jnp.int32, (1, tk), 1) // CHUNK)

    def scores(j, buf, diagonal=None):
        s_ref, mb_ref = buf
        kb = k_ref[pl.ds(pl.multiple_of(j * tk, tk), tk), :]
        s = _dot(kb, qbd_ref[...])
        if diagonal is None:
            s_ref[...] = s
            mb_ref[...] = jnp.max(s, axis=0, keepdims=True)
            return
        for g in range(2 * tq // tk):
            cols = slice(g * tk, (g + 1) * tk)
            group = g % (tq // tk)
            piece = s[:, cols]
            if group < diagonal:
                piece = jnp.full_like(piece, NEG)
            elif group == diagonal:
                piece = jnp.where(causal, piece, NEG)
            s_ref[:, cols] = piece
            mb_ref[:, cols] = jnp.max(piece, axis=0, keepdims=True)

    def update(j, buf):
        s_ref, mb_ref = buf
        vt = vt_ref[0, j]
        for c in range(2 * tq // MXU_DIM):
            cols = slice(c * MXU_DIM, (c + 1) * MXU_DIM)
            m_old = m_ref[:, cols]
            m_new = jnp.maximum(m_old, mb_ref[:, cols])
            alpha = jnp.exp2(m_old - m_new)
            p = jnp.exp2(s_ref[:, cols] - m_new)
            acc_ref[:, cols] = acc_ref[:, cols] * alpha + _dot(vt, p.astype(BF16))
            m_ref[:, cols] = m_new

    assert tq == 2 * tk
    d0 = qi * (tq // tk)
    scores(d0, bufs[0], diagonal=0)
    scores(d0 + 1, bufs[1], diagonal=1)
    update(d0, bufs[0])

    def two_steps(i, pending):
        j = 2 * i
        scores(j, bufs[0])
        update(pending, bufs[1])
        scores(j + 1, bufs[1])
        update(j, bufs[0])
        return j + 1

    pending = lax.fori_loop(0, qi, two_steps, d0 + 1)
    update(pending, bufs[1])

    a = acc_ref[:HEAD_W, :] / acc_ref[HEAD_W:HEAD_W + 1, :]
    lam = _lambda(lq1_ref, lk1_ref, lq2_ref, lk2_ref, lam_init)
    o_t = a[:, :tq] - lam * a[:, tq:]
    ms = jnp.mean(o_t * o_t, axis=0, keepdims=True)
    o_t = o_t * lax.rsqrt(ms + EPS) * sg_ref[...] * (1.0 - lam_init)
    o_ref[...] = o_t.T.astype(BF16)


def _attn_prompt(q, kb, vt, lams, sg_col, lam_init):
    seq = q.shape[0]
    tq, tk = ATTN_Q_TILE, ATTN_K_TILE
    assert vt.shape == (N_HEADS, seq // tk, VT_ROWS, tk)
    lam_spec = pl.BlockSpec((1, HEAD_DIM), lambda h, i: (0, 0))
    stat = pltpu.VMEM((1, 2 * tq), F32)
    return pl.pallas_call(
        functools.partial(_attn_prompt_kernel, tq=tq, tk=tk, lam_init=lam_init),
        grid=(N_HEADS, seq // tq),
        in_specs=[pl.BlockSpec((tq, HEAD_W), lambda h, i: (i, h)),
                  pl.BlockSpec((seq, HEAD_W), lambda h, i: (0, h)),
                  pl.BlockSpec((1, seq // tk, VT_ROWS, tk), lambda h, i: (h, 0, 0, 0)),
                  lam_spec, lam_spec, lam_spec, lam_spec,
                  pl.BlockSpec((HEAD_W, 1), lambda h, i: (0, 0))],
        out_specs=pl.BlockSpec((tq, HEAD_W), lambda h, i: (i, h)),
        out_shape=jax.ShapeDtypeStruct((seq, N_HEADS * HEAD_W), BF16),
        scratch_shapes=[pltpu.VMEM((HEAD_W, 2 * tq), BF16),
                        pltpu.VMEM((tk, 2 * tq), F32),
                        pltpu.VMEM((tk, 2 * tq), F32),
                        stat, stat, stat,
                        pltpu.VMEM((VT_ROWS, 2 * tq), F32)],
        compiler_params=_params(2), name="attn_prompt",
    )(q, kb, vt, *lams, sg_col)


def _attn_sample_kernel(q_ref, ck_ref, cv_ref, kn_ref, vn_ref, lq1_ref, lk1_ref,
                        lq2_ref, lk2_ref, sg_ref, o_ref, *, t, lam_init):
    q = q_ref[...]
    row = lax.broadcasted_iota(jnp.int32, (2 * t, HEAD_W), 0)
    col = lax.broadcasted_iota(jnp.int32, (2 * t, HEAD_W), 1)
    qq = jnp.concatenate([q, q], axis=0)
    qbd = jnp.where((row < t) == (col < HEAD_DIM), qq, jnp.zeros_like(qq))
    nt = (((1,), (1,)), ((), ()))
    s_c = _dot(qbd, ck_ref[0].astype(BF16))
    s_n = lax.dot_general(qbd, kn_ref[...].astype(BF16), nt, preferred_element_type=F32)
    m = jnp.maximum(jnp.max(s_c, axis=-1, keepdims=True), jnp.max(s_n, axis=-1, keepdims=True))
    p_c = jnp.exp2(s_c - m)
    p_n = jnp.exp2(s_n - m)
    l = jnp.sum(p_c, axis=-1, keepdims=True) + jnp.sum(p_n, axis=-1, keepdims=True)
    past = cv_ref.shape[1] // N_HEADS
    v_c = cv_ref[0, pl.ds(pl.program_id(1), past, stride=N_HEADS), :]
    a = (_dot(p_c.astype(BF16), v_c.astype(BF16))
         + _dot(p_n.astype(BF16), vn_ref[...].astype(BF16))) / l
    lam = _lambda(lq1_ref, lk1_ref, lq2_ref, lk2_ref, lam_init)
    o = a[:t] - lam * a[t:]
    o = _rms(o, sg_ref[...]) * (1.0 - lam_init)
    o_ref[...] = o.astype(BF16)


def _attn_sample(q, cache_kt, cache_v, k_new, v_new, lams, sg_row, lam_init, *, t):
    n_streams, past = cache_kt.shape[0], cache_kt.shape[2]
    new_spec = pl.BlockSpec((t, HEAD_W), lambda b, h: (b, h))
    cache_spec = pl.BlockSpec((1, HEAD_W, past), lambda b, h: (b, h, 0))
    value_spec = pl.BlockSpec((1, past * N_HEADS, HEAD_W), lambda b, h: (b, 0, 0))
    lam_spec = pl.BlockSpec((1, HEAD_DIM), lambda b, h: (0, 0))
    return pl.pallas_call(
        functools.partial(_attn_sample_kernel, t=t, lam_init=lam_init),
        grid=(n_streams, N_HEADS),
        in_specs=[new_spec, cache_spec, value_spec, new_spec, new_spec,
                  lam_spec, lam_spec, lam_spec, lam_spec,
                  pl.BlockSpec((1, HEAD_W), lambda b, h: (0, 0))],
        out_specs=new_spec,
        out_shape=jax.ShapeDtypeStruct((n_streams * t, N_HEADS * HEAD_W), BF16),
        compiler_params=_params(2), name="attn_sample",
    )(q, cache_kt, cache_v, k_new, v_new, *lams, sg_row)


def kernel(x_prompt, x_sample, cache_k, cache_v, state_conv, ffn_norm, ffn_w_gate, ffn_w_up, ffn_w_down, conv_norm, conv_w_pw1, conv_b_pw1, conv_w_dw, conv_b_dw, conv_ln_g, conv_ln_b, conv_w_pw2, conv_b_pw2, kv_norm, w_k, w_v, k_norm, attn_norm, w_q, q_norm, lambda_q1, lambda_k1, lambda_q2, lambda_k2, subln, w_o):
    batch, seq, _ = x_prompt.shape
    dec_batch, dec_seq, _ = x_sample.shape
    assert batch == 1 and seq % ATTN_Q_TILE == 0 and dec_seq % CONV_ROWS == 0
    assert ffn_norm.shape[0] == 2 and conv_norm.shape[0] == 1 and attn_norm.shape[0] == 1
    past = cache_k.shape[1]
    qk = N_HEADS * HEAD_W
    row = lambda v: v.reshape(1, -1).astype(F32)
    bf = lambda w: w.astype(BF16)

    ffn_w = [[(row(ffn_norm[l, i]), bf(ffn_w_gate[l, i]), bf(ffn_w_up[l, i]), bf(ffn_w_down[l, i]))
              for i in range(2)] for l in range(2)]
    sub = lambda v: jnp.broadcast_to(v.astype(F32)[..., None, :],
                                     v.shape[:-1] + (SUBLANES, v.shape[-1]))
    conv_w = (row(conv_norm[0]), bf(conv_w_pw1[0]), row(conv_b_pw1[0]), sub(conv_w_dw[0]),
              sub(conv_b_dw[0]), row(conv_ln_g[0]), row(conv_ln_b[0]), bf(conv_w_pw2[0]),
              row(conv_b_pw2[0]))
    kv_w = (row(kv_norm), bf(w_k), bf(w_v), row(jnp.tile(k_norm, qk // HEAD_DIM)))
    q_w = (row(attn_norm[0]), bf(w_q[0]), row(jnp.tile(q_norm[0], qk // HEAD_DIM)))
    lams = (row(lambda_q1[0]), row(lambda_k1[0]), row(lambda_q2[0]), row(lambda_k2[0]))
    lam_init = 0.8 - 0.6 * math.exp(-0.3 * 1)
    sg = subln[0].astype(F32)
    wo = bf(w_o[0])

    def layer0(x, hist, nb, t):
        x = _ffn(x, *ffn_w[0][0])
        x, new_hist = _conv(x, hist, *conv_w, nb=nb, t=t)
        x = _ffn(x, *ffn_w[0][1])
        return x, new_hist, _kv(x, *kv_w)

    xp = x_prompt.reshape(seq, D_MODEL)
    zero_hist = jnp.zeros((1, CONV_HIST, CONV_C), F32)
    xp, hist_p, (k_p, v_p, kb_p, vt_p) = layer0(xp, zero_hist, 1, ROW_TILE)
    xp = _ffn(xp, *ffn_w[1][0])
    a_p = _attn_prompt(_q(xp, *q_w), kb_p, vt_p, lams, sg.reshape(HEAD_W, 1), lam_init)
    yp = _ffn(xp, *ffn_w[1][1], attn=a_p, wo=wo)

    xs = x_sample.reshape(dec_batch * dec_seq, D_MODEL)
    nb_s = 8
    xs, hist_s, (k_s, v_s, _, _) = layer0(xs, state_conv[0], nb_s, dec_seq)
    xs = _ffn(xs, *ffn_w[1][0])
    cache_kt = jnp.transpose(cache_k, (0, 2, 3, 4, 1)).reshape(dec_batch, qk, past)
    a_s = _attn_sample(_q(xs, *q_w), cache_kt,
                       cache_v.reshape(dec_batch, past * N_HEADS, HEAD_W), k_s, v_s, lams,
                       sg.reshape(1, HEAD_W), lam_init, t=dec_seq)
    ys = _ffn(xs, *ffn_w[1][1], attn=a_s, wo=wo)

    return (yp.reshape(1, seq, D_MODEL),
            ys.reshape(dec_batch, dec_seq, D_MODEL),
            k_p.reshape(1, seq, N_HEADS, 2, HEAD_DIM),
            v_p.reshape(1, seq, N_HEADS, HEAD_W),
            hist_p.reshape(1, 1, CONV_HIST, CONV_C),
            k_s.reshape(dec_batch, dec_seq, N_HEADS, 2, HEAD_DIM),
            v_s.reshape(dec_batch, dec_seq, N_HEADS, HEAD_W),
            hist_s.reshape(1, dec_batch, CONV_HIST, CONV_C))
```

```python
import functools
import math

import jax
import jax.numpy as jnp
from jax import lax
from jax.experimental import pallas as pl
from jax.experimental.pallas import tpu as pltpu

D_MODEL = 1024
D_FF = 2816
CONV_C = 1024
CONV_W = 31
CONV_HIST = CONV_W - 1
N_HEADS = 8
HEAD_DIM = 64
HEAD_W = 2 * HEAD_DIM
VT_ROWS = HEAD_W + 16
CHUNK = 64
EPS = 1e-6
SCALE = 1.0 / math.sqrt(HEAD_DIM)
LOG2E = math.log2(math.e)
NEG = -1e30

F32 = jnp.float32
BF16 = jnp.bfloat16

SUBLANES = 8
LANES = 128
MXU_DIM = 256
ROW_TILE = 512
ATTN_K_TILE = ROW_TILE
ATTN_Q_TILE = 2 * ATTN_K_TILE
CONV_PAD = 32
CONV_ROWS = 32
VMEM_LIMIT_BYTES = 56 * 1024 * 1024


def _params(n_grid_dims):
    return pltpu.CompilerParams(
        dimension_semantics=("arbitrary",) * n_grid_dims,
        vmem_limit_bytes=VMEM_LIMIT_BYTES)


def _const_spec(shape):
    zeros = (0,) * len(shape)
    return pl.BlockSpec(shape, lambda *_: zeros, pipeline_mode=pl.Buffered(1))


def _rows_spec(tm, width):
    return pl.BlockSpec((tm, width), lambda i: (i, 0))


def _dot(a, b):
    return jnp.dot(a, b, preferred_element_type=F32)


def _rms(x, g):
    return x * lax.rsqrt(jnp.mean(x * x, axis=-1, keepdims=True) + EPS) * g


def _group_rms(y, g_tiled):
    r = lax.broadcasted_iota(jnp.int32, (MXU_DIM, MXU_DIM), 0) // HEAD_DIM
    c = lax.broadcasted_iota(jnp.int32, (MXU_DIM, MXU_DIM), 1) // HEAD_DIM
    ones_bd = (r == c).astype(BF16)
    outs = []
    for j in range(y.shape[1] // MXU_DIM):
        ys = y[:, j * MXU_DIM:(j + 1) * MXU_DIM]
        ss = _dot((ys * ys).astype(BF16), ones_bd)
        outs.append(ys * lax.rsqrt(ss * (1.0 / HEAD_DIM) + EPS))
    return jnp.concatenate(outs, axis=1) * g_tiled


def _ffn_body(x, g_ref, wg_ref, wu_ref, wd_ref):
    h = _rms(x, g_ref[...]).astype(BF16)
    gate = _dot(h, wg_ref[...])
    up = _dot(h, wu_ref[...])
    act = (gate * jax.nn.sigmoid(gate) * up).astype(BF16)
    return x + 0.5 * _dot(act, wd_ref[...])


def _ffn_rows(ins, consts, outs):
    (x_ref,), (o_ref,) = ins, outs
    o_ref[...] = _ffn_body(x_ref[...], *consts)


def _proj_ffn_rows(ins, consts, outs):
    (x_ref, a_ref), (o_ref,) = ins, outs
    wo_ref, *ffn_consts = consts
    x = x_ref[...] + _dot(a_ref[...], wo_ref[...])
    o_ref[...] = _ffn_body(x, *ffn_consts)


def _row_out(rows, width, dtype):
    return (jax.ShapeDtypeStruct((rows, width), dtype), (ROW_TILE, width), lambda t: (t, 0))


def _rowwise_call(body, name, prompt_ins, sample_ins, consts, const_specs,
                  prompt_outs, sample_outs):
    n_p = prompt_ins[0].shape[0] // ROW_TILE
    assert all(a.shape[0] == ROW_TILE for a in sample_ins)
    prompt_tile = lambda i: jnp.minimum(i, n_p - 1)
    sample_tile = lambda i: 0
    n_in, n_const, n_po = len(prompt_ins), len(consts), len(prompt_outs)

    def in_spec(a, tile):
        return pl.BlockSpec((ROW_TILE, a.shape[1]), lambda i: (tile(i), 0))

    def out_spec(o, tile):
        _, block, index = o
        return pl.BlockSpec(block, lambda i: index(tile(i)))

    def kernel(*refs):
        p_in, s_in = refs[:n_in], refs[n_in:2 * n_in]
        const_refs = refs[2 * n_in:2 * n_in + n_const]
        out_refs = refs[2 * n_in + n_const:]
        i = pl.program_id(0)

        @pl.when(i < n_p)
        def _():
            body(p_in, const_refs, out_refs[:n_po])

        @pl.when(i >= n_p)
        def _():
            body(s_in, const_refs, out_refs[n_po:])

    outs = pl.pallas_call(
        kernel, grid=(n_p + 1,),
        in_specs=([in_spec(a, prompt_tile) for a in prompt_ins]
                  + [in_spec(a, sample_tile) for a in sample_ins] + list(const_specs)),
        out_specs=([out_spec(o, prompt_tile) for o in prompt_outs]
                   + [out_spec(o, sample_tile) for o in sample_outs]),
        out_shape=[o[0] for o in prompt_outs] + [o[0] for o in sample_outs],
        compiler_params=_params(1), name=name,
    )(*prompt_ins, *sample_ins, *consts)
    return outs[:n_po], outs[n_po:]


def _layer_spec(shape, layer, index):
    tail = (0,) * len(shape)
    return pl.BlockSpec((None, None) + tuple(shape), lambda i: (layer, index) + tail,
                        pipeline_mode=pl.Buffered(1))


def _ffn(xp, xs, ffn_params, layer, index, attn=None, wo=None):
    consts = list(ffn_params)
    const_specs = [_layer_spec((1, D_MODEL), layer, index),
                   _layer_spec((D_MODEL, D_FF), layer, index),
                   _layer_spec((D_MODEL, D_FF), layer, index),
                   _layer_spec((D_FF, D_MODEL), layer, index)]
    p_ins, s_ins, body, name = [xp], [xs], _ffn_rows, "ffn"
    if attn is not None:
        p_ins, s_ins, body, name = [xp, attn[0]], [xs, attn[1]], _proj_ffn_rows, "proj_ffn"
        consts = [wo] + consts
        const_specs = [_const_spec((D_MODEL, D_MODEL))] + const_specs
    (yp,), (ys,) = _rowwise_call(
        body, name, p_ins, s_ins, consts, const_specs,
        [_row_out(xp.shape[0], D_MODEL, F32)], [_row_out(xs.shape[0], D_MODEL, F32)])
    return yp, ys


def _conv_kernel(x_ref, hist_ref, g_ref, w1_ref, b1_ref, wdw_ref, bdw_ref,
                 lng_ref, lnb_ref, w2_ref, b2_ref, o_ref, hist_out_ref,
                 shift_ref, c_ref, *, nb, t, carry):
    i = pl.program_id(0)
    length = CONV_PAD + t
    x = x_ref[...]
    h = _rms(x, g_ref[...]).astype(BF16)
    u = _dot(h, w1_ref[...]) + b1_ref[...]
    glu = u[:, :CONV_C] * jax.nn.sigmoid(u[:, CONV_C:])

    lead = CONV_PAD - CONV_HIST
    if carry:
        @pl.when(i == 0)
        def _():
            shift_ref[0, :, lead:CONV_PAD, :] = hist_ref[...]
    else:
        shift_ref[0, :, lead:CONV_PAD, :] = hist_ref[...]
    shift_ref[0, :, 0:lead, :] = jnp.zeros((nb, lead, CONV_C), F32)
    shift_ref[0, :, CONV_PAD:, :] = glu.reshape(nb, t, CONV_C)
    for s in range(1, SUBLANES):
        shift_ref[s, :, 0:length - SUBLANES, :] = shift_ref[0, :, s:s + length - SUBLANES, :]

    for lt in range(CONV_C // LANES):
        lanes = slice(lt * LANES, (lt + 1) * LANES)
        tap_w = [wdw_ref[w, :, lanes] for w in range(CONV_W)]
        bias = bdw_ref[:, lanes]

        def conv_rows(b, j, lanes=lanes, tap_w=tap_w, bias=bias):
            base = pl.multiple_of(j * CONV_ROWS, CONV_ROWS)
            frames = {}

            def window(off):
                if off not in frames:
                    s = off % SUBLANES
                    frames[off] = shift_ref[s, b, pl.ds(base + (off - s), SUBLANES), lanes]
                return frames[off]

            for k in range(CONV_ROWS // SUBLANES):
                acc = [bias, None]
                for w in range(CONV_W):
                    term = window(k * SUBLANES + lead + w) * tap_w[w]
                    acc[w % 2] = term if acc[w % 2] is None else acc[w % 2] + term
                rows = pl.ds(pl.multiple_of(b * t + base + k * SUBLANES, SUBLANES), SUBLANES)
                c_ref[rows, lanes] = acc[0] + acc[1]

        def conv_stream(b, carry_val, conv_rows=conv_rows):
            def conv_step(j, c):
                conv_rows(b, j)
                return c
            return lax.fori_loop(0, t // CONV_ROWS, conv_step, carry_val)

        lax.fori_loop(0, nb, conv_stream, 0)

    c = c_ref[...]
    mu = jnp.mean(c, axis=-1, keepdims=True)
    cc = c - mu
    var = jnp.mean(cc * cc, axis=-1, keepdims=True)
    y = cc * lax.rsqrt(var + EPS) * lng_ref[...] + lnb_ref[...]
    y = (y * jax.nn.sigmoid(y)).astype(BF16)
    o_ref[...] = x + _dot(y, w2_ref[...]) + b2_ref[...]

    tail = shift_ref[0, :, length - CONV_HIST:length, :]
    hist_out_ref[...] = tail
    if carry:
        shift_ref[0, :, lead:CONV_PAD, :] = tail


def _conv(x, hist, g, w1, b1, wdw, bdw, lng, lnb, w2, b2, *, nb, t):
    rows = x.shape[0]
    n_streams = hist.shape[0]
    tm = nb * t
    carry = nb == 1 and rows // n_streams > t
    if carry:
        assert n_streams == 1
        hist_map = lambda i: (0, 0, 0)
    else:
        assert rows == n_streams * t
        hist_map = lambda i: (i, 0, 0)
    vec = lambda n: _const_spec((1, n))
    return pl.pallas_call(
        functools.partial(_conv_kernel, nb=nb, t=t, carry=carry),
        grid=(rows // tm,),
        in_specs=[_rows_spec(tm, D_MODEL),
                  pl.BlockSpec((nb, CONV_HIST, CONV_C), hist_map),
                  vec(D_MODEL), _const_spec((D_MODEL, 2 * CONV_C)), vec(2 * CONV_C),
                  _const_spec((CONV_W, SUBLANES, CONV_C)), _const_spec((SUBLANES, CONV_C)),
                  vec(CONV_C), vec(CONV_C),
                  _const_spec((CONV_C, D_MODEL)), vec(D_MODEL)],
        out_specs=[_rows_spec(tm, D_MODEL),
                   pl.BlockSpec((nb, CONV_HIST, CONV_C), hist_map)],
        out_shape=[jax.ShapeDtypeStruct((rows, D_MODEL), F32),
                   jax.ShapeDtypeStruct((n_streams, CONV_HIST, CONV_C), F32)],
        scratch_shapes=[pltpu.VMEM((SUBLANES, nb, CONV_PAD + t, CONV_C), F32),
                        pltpu.VMEM((tm, CONV_C), F32)],
        compiler_params=_params(1), name="conv_module",
    )(x, hist, g, w1, b1, wdw, bdw, lng, lnb, w2, b2)


def _kv_rows(ins, consts, outs):
    (x_ref,), (g_ref, wk_ref, wv_ref, kg_ref) = ins, consts
    h = _rms(x_ref[...], g_ref[...]).astype(BF16)
    k = _group_rms(_dot(h, wk_ref[...]), kg_ref[...])
    v = _dot(h, wv_ref[...])
    outs[0][...] = k
    outs[1][...] = v
    if len(outs) == 4:
        kb_ref, vt_ref = outs[2:]
        kb_ref[...] = k.astype(BF16)
        tm = v.shape[0]
        vt_ref[:, 0, :HEAD_W, :] = v.T.astype(BF16).reshape(N_HEADS, HEAD_W, tm)
        vt_ref[:, 0, HEAD_W:, :] = jnp.ones((N_HEADS, VT_ROWS - HEAD_W, tm), BF16)


def _kv(xp, xs, g, wk, wv, kg):
    qk = N_HEADS * HEAD_W
    n_p = xp.shape[0] // ROW_TILE
    vt_out = (jax.ShapeDtypeStruct((N_HEADS, n_p, VT_ROWS, ROW_TILE), BF16),
              (N_HEADS, 1, VT_ROWS, ROW_TILE), lambda t: (0, t, 0, 0))
    return _rowwise_call(
        _kv_rows, "kv_proj", [xp], [xs], [g, wk, wv, kg],
        [_const_spec((1, D_MODEL)), _const_spec((D_MODEL, qk)), _const_spec((D_MODEL, qk)),
         _const_spec((1, qk))],
        [_row_out(xp.shape[0], qk, F32), _row_out(xp.shape[0], qk, F32),
         _row_out(xp.shape[0], qk, BF16), vt_out],
        [_row_out(xs.shape[0], qk, F32), _row_out(xs.shape[0], qk, F32)])


def _q_rows(ins, consts, outs):
    (x_ref,), (g_ref, wq_ref, qg_ref), (q_ref,) = ins, consts, outs
    h = _rms(x_ref[...], g_ref[...]).astype(BF16)
    q = _group_rms(_dot(h, wq_ref[...]), qg_ref[...]) * (SCALE * LOG2E)
    q_ref[...] = q.astype(BF16)


def _q(xp, xs, g, wq, qg):
    qk = N_HEADS * HEAD_W
    (qp,), (qs,) = _rowwise_call(
        _q_rows, "q_proj", [xp], [xs], [g, wq, qg],
        [_const_spec((1, D_MODEL)), _const_spec((D_MODEL, qk)), _const_spec((1, qk))],
        [_row_out(xp.shape[0], qk, BF16)], [_row_out(xs.shape[0], qk, BF16)])
    return qp, qs


def _lambda(lq1_ref, lk1_ref, lq2_ref, lk2_ref, lam_init):
    s1 = jnp.sum(lq1_ref[...] * lk1_ref[...], axis=-1, keepdims=True)
    s2 = jnp.sum(lq2_ref[...] * lk2_ref[...], axis=-1, keepdims=True)
    return jnp.exp(s1) - jnp.exp(s2) + lam_init


def _attn_prompt_kernel(q_ref, k_ref, vt_ref, lq1_ref, lk1_ref, lq2_ref, lk2_ref,
                        sg_ref, o_ref, qbd_ref, s0_ref, s1_ref, mb0_ref, mb1_ref,
                        p0_ref, p1_ref, alpha0_ref, alpha1_ref,
                        m_ref, acc_ref, *, tq, tk, lam_init):
    qi = pl.program_id(1)
    q_t = q_ref[...].astype(F32).T
    row = lax.broadcasted_iota(jnp.int32, (HEAD_W, 2 * tq), 0)
    col = lax.broadcasted_iota(jnp.int32, (HEAD_W, 2 * tq), 1)
    qq = jnp.concatenate([q_t, q_t], axis=1)
    qbd_ref[...] = jnp.where((row < HEAD_DIM) == (col < tq), qq, 0.0).astype(BF16)
    m_ref[...] = jnp.full(m_ref.shape, NEG, F32)
    acc_ref[...] = jnp.zeros(acc_ref.shape, F32)
    bufs = ((s0_ref, mb0_ref), (s1_ref, mb1_ref))
    pbufs = ((p0_ref, alpha0_ref), (p1_ref, alpha1_ref))
    causal = (lax.broadcasted_iota(jnp.int32, (tk, 1), 0) // CHUNK
              <= lax.broadcasted_iota(jnp.int32, (1, tk), 1) // CHUNK)

    def scores(j, buf, diagonal=None):
        s_ref, mb_ref = buf
        kb = k_ref[pl.ds(pl.multiple_of(j * tk, tk), tk), :]
        s = _dot(kb, qbd_ref[...])
        if diagonal is None:
            s_ref[...] = s
            mb_ref[...] = jnp.max(s, axis=0, keepdims=True)
            return
        for g in range(2 * tq // tk):
            cols = slice(g * tk, (g + 1) * tk)
            group = g % (tq // tk)
            piece = s[:, cols]
            if group < diagonal:
                piece = jnp.full_like(piece, NEG)
            elif group == diagonal:
                piece = jnp.where(causal, piece, NEG)
            s_ref[:, cols] = piece
            mb_ref[:, cols] = jnp.max(piece, axis=0, keepdims=True)

    strips = [slice(c * MXU_DIM, (c + 1) * MXU_DIM) for c in range(2 * tq // MXU_DIM)]

    def softmax(buf, pbuf):
        s_ref, mb_ref = buf
        p_ref, alpha_ref = pbuf
        for cols in strips:
            m_old = m_ref[:, cols]
            m_new = jnp.maximum(m_old, mb_ref[:, cols])
            alpha_ref[:, cols] = jnp.exp2(m_old - m_new)
            p_ref[:, cols] = jnp.exp2(s_ref[:, cols] - m_new).astype(BF16)
            m_ref[:, cols] = m_new

    def accumulate(j, pbuf):
        p_ref, alpha_ref = pbuf
        vt = vt_ref[0, j]
        for cols in strips:
            acc_ref[:, cols] = acc_ref[:, cols] * alpha_ref[:, cols] + _dot(vt, p_ref[:, cols])

    assert tq == 2 * tk
    d0 = qi * (tq // tk)
    scores(d0, bufs[0], diagonal=0)
    scores(d0 + 1, bufs[1], diagonal=1)
    softmax(bufs[0], pbufs[0])

    def two_steps(i, carry):
        even_block = jnp.where(i == 0, d0, 2 * i - 2)
        scores(2 * i, bufs[0])
        softmax(bufs[1], pbufs[1])
        accumulate(even_block, pbufs[0])
        scores(2 * i + 1, bufs[1])
        softmax(bufs[0], pbufs[0])
        accumulate(even_block + 1, pbufs[1])
        return carry

    lax.fori_loop(0, qi, two_steps, 0)
    last_even = jnp.where(qi == 0, d0, 2 * qi - 2)
    softmax(bufs[1], pbufs[1])
    accumulate(last_even, pbufs[0])
    accumulate(last_even + 1, pbufs[1])

    a = acc_ref[:HEAD_W, :] / acc_ref[HEAD_W:HEAD_W + 1, :]
    lam = _lambda(lq1_ref, lk1_ref, lq2_ref, lk2_ref, lam_init)
    o_t = a[:, :tq] - lam * a[:, tq:]
    ms = jnp.mean(o_t * o_t, axis=0, keepdims=True)
    o_t = o_t * lax.rsqrt(ms + EPS) * sg_ref[...] * (1.0 - lam_init)
    o_ref[...] = o_t.T.astype(BF16)


def _attn_prompt(q, kb, vt, lams, sg_col, lam_init):
    seq = q.shape[0]
    tq, tk = ATTN_Q_TILE, ATTN_K_TILE
    assert vt.shape == (N_HEADS, seq // tk, VT_ROWS, tk)
    lam_spec = pl.BlockSpec((1, HEAD_DIM), lambda h, i: (0, 0))
    stat = pltpu.VMEM((1, 2 * tq), F32)
    return pl.pallas_call(
        functools.partial(_attn_prompt_kernel, tq=tq, tk=tk, lam_init=lam_init),
        grid=(N_HEADS, seq // tq),
        in_specs=[pl.BlockSpec((tq, HEAD_W), lambda h, i: (i, h)),
                  pl.BlockSpec((seq, HEAD_W), lambda h, i: (0, h)),
                  pl.BlockSpec((1, seq // tk, VT_ROWS, tk), lambda h, i: (h, 0, 0, 0)),
                  lam_spec, lam_spec, lam_spec, lam_spec,
                  pl.BlockSpec((HEAD_W, 1), lambda h, i: (0, 0))],
        out_specs=pl.BlockSpec((tq, HEAD_W), lambda h, i: (i, h)),
        out_shape=jax.ShapeDtypeStruct((seq, N_HEADS * HEAD_W), BF16),
        scratch_shapes=[pltpu.VMEM((HEAD_W, 2 * tq), BF16),
                        pltpu.VMEM((tk, 2 * tq), F32),
                        pltpu.VMEM((tk, 2 * tq), F32),
                        stat, stat,
                        pltpu.VMEM((tk, 2 * tq), BF16),
                        pltpu.VMEM((tk, 2 * tq), BF16),
                        stat, stat,
                        stat,
                        pltpu.VMEM((VT_ROWS, 2 * tq), F32)],
        compiler_params=_params(2), name="attn_prompt",
    )(q, kb, vt, *lams, sg_col)


def _attn_sample_kernel(q_ref, ck_ref, cv_ref, kn_ref, vn_ref, lq1_ref, lk1_ref,
                        lq2_ref, lk2_ref, sg_ref, o_ref, *, t, lam_init):
    row = lax.broadcasted_iota(jnp.int32, (2 * t, HEAD_W), 0)
    col = lax.broadcasted_iota(jnp.int32, (2 * t, HEAD_W), 1)
    block_diag = (row < t) == (col < HEAD_DIM)
    nt = (((1,), (1,)), ((), ()))
    past = cv_ref.shape[1] // N_HEADS
    lam = _lambda(lq1_ref, lk1_ref, lq2_ref, lk2_ref, lam_init)
    for h in range(N_HEADS):
        lanes = slice(h * HEAD_W, (h + 1) * HEAD_W)
        q = q_ref[:, lanes]
        qq = jnp.concatenate([q, q], axis=0)
        qbd = jnp.where(block_diag, qq, jnp.zeros_like(qq))
        s_c = _dot(qbd, ck_ref[0, lanes, :].astype(BF16))
        s_n = lax.dot_general(qbd, kn_ref[:, lanes].astype(BF16), nt,
                              preferred_element_type=F32)
        m = jnp.maximum(jnp.max(s_c, axis=-1, keepdims=True),
                        jnp.max(s_n, axis=-1, keepdims=True))
        p_c = jnp.exp2(s_c - m)
        p_n = jnp.exp2(s_n - m)
        l = jnp.sum(p_c, axis=-1, keepdims=True) + jnp.sum(p_n, axis=-1, keepdims=True)
        v_c = cv_ref[0, pl.ds(h, past, stride=N_HEADS), :]
        a = (_dot(p_c.astype(BF16), v_c.astype(BF16))
             + _dot(p_n.astype(BF16), vn_ref[:, lanes].astype(BF16))) / l
        o = a[:t] - lam * a[t:]
        o = _rms(o, sg_ref[...]) * (1.0 - lam_init)
        o_ref[:, lanes] = o.astype(BF16)


def _attn_sample(q, cache_kt, cache_v, k_new, v_new, lams, sg_row, lam_init, *, t):
    n_streams, width, past = cache_kt.shape
    new_spec = pl.BlockSpec((t, width), lambda b: (b, 0))
    lam_spec = pl.BlockSpec((1, HEAD_DIM), lambda b: (0, 0))
    return pl.pallas_call(
        functools.partial(_attn_sample_kernel, t=t, lam_init=lam_init),
        grid=(n_streams,),
        in_specs=[new_spec,
                  pl.BlockSpec((1, width, past), lambda b: (b, 0, 0)),
                  pl.BlockSpec((1, past * N_HEADS, HEAD_W), lambda b: (b, 0, 0)),
                  new_spec, new_spec,
                  lam_spec, lam_spec, lam_spec, lam_spec,
                  pl.BlockSpec((1, HEAD_W), lambda b: (0, 0))],
        out_specs=new_spec,
        out_shape=jax.ShapeDtypeStruct((n_streams * t, width), BF16),
        compiler_params=_params(1), name="attn_sample",
    )(q, cache_kt, cache_v, k_new, v_new, *lams, sg_row)


def kernel(x_prompt, x_sample, cache_k, cache_v, state_conv, ffn_norm, ffn_w_gate, ffn_w_up, ffn_w_down, conv_norm, conv_w_pw1, conv_b_pw1, conv_w_dw, conv_b_dw, conv_ln_g, conv_ln_b, conv_w_pw2, conv_b_pw2, kv_norm, w_k, w_v, k_norm, attn_norm, w_q, q_norm, lambda_q1, lambda_k1, lambda_q2, lambda_k2, subln, w_o):
    batch, seq, _ = x_prompt.shape
    dec_batch, dec_seq, _ = x_sample.shape
    assert batch == 1 and seq % ATTN_Q_TILE == 0 and dec_seq % CONV_ROWS == 0
    assert ffn_norm.shape[0] == 2 and conv_norm.shape[0] == 1 and attn_norm.shape[0] == 1
    assert dec_batch * dec_seq == ROW_TILE
    past = cache_k.shape[1]
    qk = N_HEADS * HEAD_W
    row = lambda v: v.reshape(1, -1).astype(F32)
    bf = lambda w: w.astype(BF16)

    ffn_params = (ffn_norm.astype(F32)[:, :, None, :], bf(ffn_w_gate), bf(ffn_w_up), bf(ffn_w_down))
    sub = lambda v: jnp.broadcast_to(v.astype(F32)[..., None, :],
                                     v.shape[:-1] + (SUBLANES, v.shape[-1]))
    conv_w = (row(conv_norm[0]), bf(conv_w_pw1[0]), row(conv_b_pw1[0]), sub(conv_w_dw[0]),
              sub(conv_b_dw[0]), row(conv_ln_g[0]), row(conv_ln_b[0]), bf(conv_w_pw2[0]),
              row(conv_b_pw2[0]))
    kv_w = (row(kv_norm), bf(w_k), bf(w_v), row(jnp.tile(k_norm, qk // HEAD_DIM)))
    q_w = (row(attn_norm[0]), bf(w_q[0]), row(jnp.tile(q_norm[0], qk // HEAD_DIM)))
    lams = (row(lambda_q1[0]), row(lambda_k1[0]), row(lambda_q2[0]), row(lambda_k2[0]))
    lam_init = 0.8 - 0.6 * math.exp(-0.3 * 1)
    sg = subln[0].astype(F32)
    wo = bf(w_o[0])

    xp = x_prompt.reshape(seq, D_MODEL)
    xs = x_sample.reshape(dec_batch * dec_seq, D_MODEL)

    xp, xs = _ffn(xp, xs, ffn_params, 0, 0)
    zero_hist = jnp.zeros((1, CONV_HIST, CONV_C), F32)
    xp, hist_p = _conv(xp, zero_hist, *conv_w, nb=1, t=ROW_TILE)
    xs, hist_s = _conv(xs, state_conv[0], *conv_w, nb=8, t=dec_seq)
    xp, xs = _ffn(xp, xs, ffn_params, 0, 1)
    (k_p, v_p, kb_p, vt_p), (k_s, v_s) = _kv(xp, xs, *kv_w)

    xp, xs = _ffn(xp, xs, ffn_params, 1, 0)
    q_p, q_s = _q(xp, xs, *q_w)
    a_p = _attn_prompt(q_p, kb_p, vt_p, lams, sg.reshape(HEAD_W, 1), lam_init)
    cache_kt = jnp.transpose(cache_k, (0, 2, 3, 4, 1)).reshape(dec_batch, qk, past)
    a_s = _attn_sample(q_s, cache_kt, cache_v.reshape(dec_batch, past * N_HEADS, HEAD_W),
                       k_s, v_s, lams, sg.reshape(1, HEAD_W), lam_init, t=dec_seq)
    yp, ys = _ffn(xp, xs, ffn_params, 1, 1, attn=(a_p, a_s), wo=wo)

    return (yp.reshape(1, seq, D_MODEL),
            ys.reshape(dec_batch, dec_seq, D_MODEL),
            k_p.reshape(1, seq, N_HEADS, 2, HEAD_DIM),
            v_p.reshape(1, seq, N_HEADS, HEAD_W),
            hist_p.reshape(1, 1, CONV_HIST, CONV_C),
            k_s.reshape(dec_batch, dec_seq, N_HEADS, 2, HEAD_DIM),
            v_s.reshape(dec_batch, dec_seq, N_HEADS, HEAD_W),
            hist_s.reshape(1, dec_batch, CONV_HIST, CONV_C))
```

```python
import functools
import math

import jax
import jax.numpy as jnp
from jax import lax
from jax.experimental import pallas as pl
from jax.experimental.pallas import tpu as pltpu

D_MODEL = 1024
D_FF = 2816
CONV_C = 1024
CONV_W = 31
CONV_HIST = CONV_W - 1
N_HEADS = 8
HEAD_DIM = 64
HEAD_W = 2 * HEAD_DIM
VT_ROWS = HEAD_W + 16
CHUNK = 64
EPS = 1e-6
SCALE = 1.0 / math.sqrt(HEAD_DIM)
LOG2E = math.log2(math.e)
NEG = -1e30

F32 = jnp.float32
BF16 = jnp.bfloat16

SUBLANES = 8
LANES = 128
MXU_DIM = 256
ROW_TILE = 512
ATTN_K_TILE = ROW_TILE
ATTN_Q_TILE = 2 * ATTN_K_TILE
CONV_PAD = 32
CONV_ROWS = 32
VMEM_LIMIT_BYTES = 56 * 1024 * 1024


def _params(n_grid_dims):
    return pltpu.CompilerParams(
        dimension_semantics=("arbitrary",) * n_grid_dims,
        vmem_limit_bytes=VMEM_LIMIT_BYTES)


def _const_spec(shape):
    zeros = (0,) * len(shape)
    return pl.BlockSpec(shape, lambda *_: zeros, pipeline_mode=pl.Buffered(1))


def _rows_spec(tm, width):
    return pl.BlockSpec((tm, width), lambda i: (i, 0))


def _dot(a, b):
    return jnp.dot(a, b, preferred_element_type=F32)


def _rms(x, g):
    return x * lax.rsqrt(jnp.mean(x * x, axis=-1, keepdims=True) + EPS) * g


def _group_rms(y, g_tiled):
    r = lax.broadcasted_iota(jnp.int32, (MXU_DIM, MXU_DIM), 0) // HEAD_DIM
    c = lax.broadcasted_iota(jnp.int32, (MXU_DIM, MXU_DIM), 1) // HEAD_DIM
    ones_bd = (r == c).astype(BF16)
    outs = []
    for j in range(y.shape[1] // MXU_DIM):
        ys = y[:, j * MXU_DIM:(j + 1) * MXU_DIM]
        ss = _dot((ys * ys).astype(BF16), ones_bd)
        outs.append(ys * lax.rsqrt(ss * (1.0 / HEAD_DIM) + EPS))
    return jnp.concatenate(outs, axis=1) * g_tiled


def _ffn_body(x, g_ref, wg_ref, wu_ref, wd_ref):
    h = _rms(x, g_ref[...]).astype(BF16)
    gate = _dot(h, wg_ref[...])
    up = _dot(h, wu_ref[...])
    act = (gate * jax.nn.sigmoid(gate) * up).astype(BF16)
    return x + 0.5 * _dot(act, wd_ref[...])


def _ffn_rows(ins, consts, outs):
    (x_ref,), (o_ref,) = ins, outs
    o_ref[...] = _ffn_body(x_ref[...], *consts)


def _proj_ffn_rows(ins, consts, outs):
    (x_ref, a_ref), (o_ref,) = ins, outs
    wo_ref, *ffn_consts = consts
    x = x_ref[...] + _dot(a_ref[...], wo_ref[...])
    o_ref[...] = _ffn_body(x, *ffn_consts)


def _row_out(rows, width, dtype):
    return (jax.ShapeDtypeStruct((rows, width), dtype), (ROW_TILE, width), lambda t: (t, 0))


def _rowwise_call(body, name, prompt_ins, sample_ins, consts, const_specs,
                  prompt_outs, sample_outs):
    n_p = prompt_ins[0].shape[0] // ROW_TILE
    assert all(a.shape[0] == ROW_TILE for a in sample_ins)
    prompt_tile = lambda i: jnp.minimum(i, n_p - 1)
    sample_tile = lambda i: 0
    n_in, n_const, n_po = len(prompt_ins), len(consts), len(prompt_outs)

    def in_spec(a, tile):
        return pl.BlockSpec((ROW_TILE, a.shape[1]), lambda i: (tile(i), 0))

    def out_spec(o, tile):
        _, block, index = o
        return pl.BlockSpec(block, lambda i: index(tile(i)))

    def kernel(*refs):
        p_in, s_in = refs[:n_in], refs[n_in:2 * n_in]
        const_refs = refs[2 * n_in:2 * n_in + n_const]
        out_refs = refs[2 * n_in + n_const:]
        i = pl.program_id(0)

        @pl.when(i < n_p)
        def _():
            body(p_in, const_refs, out_refs[:n_po])

        @pl.when(i >= n_p)
        def _():
            body(s_in, const_refs, out_refs[n_po:])

    outs = pl.pallas_call(
        kernel, grid=(n_p + 1,),
        in_specs=([in_spec(a, prompt_tile) for a in prompt_ins]
                  + [in_spec(a, sample_tile) for a in sample_ins] + list(const_specs)),
        out_specs=([out_spec(o, prompt_tile) for o in prompt_outs]
                   + [out_spec(o, sample_tile) for o in sample_outs]),
        out_shape=[o[0] for o in prompt_outs] + [o[0] for o in sample_outs],
        compiler_params=_params(1), name=name,
    )(*prompt_ins, *sample_ins, *consts)
    return outs[:n_po], outs[n_po:]


def _layer_spec(shape, layer, index):
    tail = (0,) * len(shape)
    return pl.BlockSpec((None, None) + tuple(shape), lambda i: (layer, index) + tail,
                        pipeline_mode=pl.Buffered(1))


def _ffn(xp, xs, ffn_params, layer, index, attn=None, wo=None):
    consts = list(ffn_params)
    const_specs = [_layer_spec((1, D_MODEL), layer, index),
                   _layer_spec((D_MODEL, D_FF), layer, index),
                   _layer_spec((D_MODEL, D_FF), layer, index),
                   _layer_spec((D_FF, D_MODEL), layer, index)]
    p_ins, s_ins, body, name = [xp], [xs], _ffn_rows, "ffn"
    if attn is not None:
        p_ins, s_ins, body, name = [xp, attn[0]], [xs, attn[1]], _proj_ffn_rows, "proj_ffn"
        consts = [wo] + consts
        const_specs = [_const_spec((D_MODEL, D_MODEL))] + const_specs
    (yp,), (ys,) = _rowwise_call(
        body, name, p_ins, s_ins, consts, const_specs,
        [_row_out(xp.shape[0], D_MODEL, F32)], [_row_out(xs.shape[0], D_MODEL, F32)])
    return yp, ys


def _conv_kernel(x_ref, hist_ref, g_ref, w1_ref, b1_ref, wdw_ref, bdw_ref,
                 lng_ref, lnb_ref, w2_ref, b2_ref, o_ref, hist_out_ref,
                 shift_ref, c_ref, *, nb, t, carry):
    i = pl.program_id(0)
    length = CONV_PAD + t
    x = x_ref[...]
    h = _rms(x, g_ref[...]).astype(BF16)
    u = _dot(h, w1_ref[...]) + b1_ref[...]
    glu = u[:, :CONV_C] * jax.nn.sigmoid(u[:, CONV_C:])

    lead = CONV_PAD - CONV_HIST
    if carry:
        @pl.when(i == 0)
        def _():
            shift_ref[0, :, lead:CONV_PAD, :] = hist_ref[...]
    else:
        shift_ref[0, :, lead:CONV_PAD, :] = hist_ref[...]
    shift_ref[0, :, 0:lead, :] = jnp.zeros((nb, lead, CONV_C), F32)
    shift_ref[0, :, CONV_PAD:, :] = glu.reshape(nb, t, CONV_C)
    for s in range(1, SUBLANES):
        shift_ref[s, :, 0:length - SUBLANES, :] = shift_ref[0, :, s:s + length - SUBLANES, :]

    for lt in range(CONV_C // LANES):
        lanes = slice(lt * LANES, (lt + 1) * LANES)
        tap_w = [wdw_ref[w, :, lanes] for w in range(CONV_W)]
        bias = bdw_ref[:, lanes]

        def conv_rows(b, j, lanes=lanes, tap_w=tap_w, bias=bias):
            base = pl.multiple_of(j * CONV_ROWS, CONV_ROWS)
            frames = {}

            def window(off):
                if off not in frames:
                    s = off % SUBLANES
                    frames[off] = shift_ref[s, b, pl.ds(base + (off - s), SUBLANES), lanes]
                return frames[off]

            chains = CONV_ROWS // SUBLANES
            acc = [[bias, None] for _ in range(chains)]
            for off in range(lead, lead + CONV_W + (chains - 1) * SUBLANES):
                for k in range(chains):
                    w = off - lead - k * SUBLANES
                    if 0 <= w < CONV_W:
                        term = window(off) * tap_w[w]
                        part = acc[k][w % 2]
                        acc[k][w % 2] = term if part is None else part + term
            for k in range(chains):
                rows = pl.ds(pl.multiple_of(b * t + base + k * SUBLANES, SUBLANES), SUBLANES)
                c_ref[rows, lanes] = acc[k][0] + acc[k][1]

        def conv_stream(b, carry_val, conv_rows=conv_rows):
            def conv_step(j, c):
                conv_rows(b, j)
                return c
            return lax.fori_loop(0, t // CONV_ROWS, conv_step, carry_val)

        lax.fori_loop(0, nb, conv_stream, 0)

    c = c_ref[...]
    mu = jnp.mean(c, axis=-1, keepdims=True)
    cc = c - mu
    var = jnp.mean(cc * cc, axis=-1, keepdims=True)
    y = cc * lax.rsqrt(var + EPS) * lng_ref[...] + lnb_ref[...]
    y = (y * jax.nn.sigmoid(y)).astype(BF16)
    o_ref[...] = x + _dot(y, w2_ref[...]) + b2_ref[...]

    tail = shift_ref[0, :, length - CONV_HIST:length, :]
    hist_out_ref[...] = tail
    if carry:
        shift_ref[0, :, lead:CONV_PAD, :] = tail


def _conv(x, hist, g, w1, b1, wdw, bdw, lng, lnb, w2, b2, *, nb, t):
    rows = x.shape[0]
    n_streams = hist.shape[0]
    tm = nb * t
    carry = nb == 1 and rows // n_streams > t
    if carry:
        assert n_streams == 1
        hist_map = lambda i: (0, 0, 0)
    else:
        assert rows == n_streams * t
        hist_map = lambda i: (i, 0, 0)
    vec = lambda n: _const_spec((1, n))
    return pl.pallas_call(
        functools.partial(_conv_kernel, nb=nb, t=t, carry=carry),
        grid=(rows // tm,),
        in_specs=[_rows_spec(tm, D_MODEL),
                  pl.BlockSpec((nb, CONV_HIST, CONV_C), hist_map),
                  vec(D_MODEL), _const_spec((D_MODEL, 2 * CONV_C)), vec(2 * CONV_C),
                  _const_spec((CONV_W, SUBLANES, CONV_C)), _const_spec((SUBLANES, CONV_C)),
                  vec(CONV_C), vec(CONV_C),
                  _const_spec((CONV_C, D_MODEL)), vec(D_MODEL)],
        out_specs=[_rows_spec(tm, D_MODEL),
                   pl.BlockSpec((nb, CONV_HIST, CONV_C), hist_map)],
        out_shape=[jax.ShapeDtypeStruct((rows, D_MODEL), F32),
                   jax.ShapeDtypeStruct((n_streams, CONV_HIST, CONV_C), F32)],
        scratch_shapes=[pltpu.VMEM((SUBLANES, nb, CONV_PAD + t, CONV_C), F32),
                        pltpu.VMEM((tm, CONV_C), F32)],
        compiler_params=_params(1), name="conv_module",
    )(x, hist, g, w1, b1, wdw, bdw, lng, lnb, w2, b2)


def _kv_rows(ins, consts, outs):
    (x_ref,), (g_ref, wk_ref, wv_ref, kg_ref) = ins, consts
    h = _rms(x_ref[...], g_ref[...]).astype(BF16)
    k = _group_rms(_dot(h, wk_ref[...]), kg_ref[...])
    v = _dot(h, wv_ref[...])
    outs[0][...] = k
    outs[1][...] = v
    if len(outs) == 4:
        kb_ref, vt_ref = outs[2:]
        kb_ref[...] = k.astype(BF16)
        tm = v.shape[0]
        vt_ref[:, 0, :HEAD_W, :] = v.T.astype(BF16).reshape(N_HEADS, HEAD_W, tm)
        vt_ref[:, 0, HEAD_W:, :] = jnp.ones((N_HEADS, VT_ROWS - HEAD_W, tm), BF16)


def _kv(xp, xs, g, wk, wv, kg):
    qk = N_HEADS * HEAD_W
    n_p = xp.shape[0] // ROW_TILE
    vt_out = (jax.ShapeDtypeStruct((N_HEADS, n_p, VT_ROWS, ROW_TILE), BF16),
              (N_HEADS, 1, VT_ROWS, ROW_TILE), lambda t: (0, t, 0, 0))
    return _rowwise_call(
        _kv_rows, "kv_proj", [xp], [xs], [g, wk, wv, kg],
        [_const_spec((1, D_MODEL)), _const_spec((D_MODEL, qk)), _const_spec((D_MODEL, qk)),
         _const_spec((1, qk))],
        [_row_out(xp.shape[0], qk, F32), _row_out(xp.shape[0], qk, F32),
         _row_out(xp.shape[0], qk, BF16), vt_out],
        [_row_out(xs.shape[0], qk, F32), _row_out(xs.shape[0], qk, F32)])


def _q_rows(ins, consts, outs):
    (x_ref,), (g_ref, wq_ref, qg_ref), (q_ref,) = ins, consts, outs
    h = _rms(x_ref[...], g_ref[...]).astype(BF16)
    q = _group_rms(_dot(h, wq_ref[...]), qg_ref[...]) * (SCALE * LOG2E)
    q_ref[...] = q.astype(BF16)


def _q(xp, xs, g, wq, qg):
    qk = N_HEADS * HEAD_W
    (qp,), (qs,) = _rowwise_call(
        _q_rows, "q_proj", [xp], [xs], [g, wq, qg],
        [_const_spec((1, D_MODEL)), _const_spec((D_MODEL, qk)), _const_spec((1, qk))],
        [_row_out(xp.shape[0], qk, BF16)], [_row_out(xs.shape[0], qk, BF16)])
    return qp, qs


def _lambda(lq1_ref, lk1_ref, lq2_ref, lk2_ref, lam_init):
    s1 = jnp.sum(lq1_ref[...] * lk1_ref[...], axis=-1, keepdims=True)
    s2 = jnp.sum(lq2_ref[...] * lk2_ref[...], axis=-1, keepdims=True)
    return jnp.exp(s1) - jnp.exp(s2) + lam_init


def _attn_prompt_kernel(q_ref, k_ref, vt_ref, lq1_ref, lk1_ref, lq2_ref, lk2_ref,
                        sg_ref, o_ref, qbd_ref, s0_ref, s1_ref, mb0_ref, mb1_ref,
                        p0_ref, p1_ref, alpha0_ref, alpha1_ref,
                        m_ref, acc_ref, *, tq, tk, lam_init):
    qi = pl.program_id(1)
    q_t = q_ref[...].astype(F32).T
    row = lax.broadcasted_iota(jnp.int32, (HEAD_W, 2 * tq), 0)
    col = lax.broadcasted_iota(jnp.int32, (HEAD_W, 2 * tq), 1)
    qq = jnp.concatenate([q_t, q_t], axis=1)
    qbd_ref[...] = jnp.where((row < HEAD_DIM) == (col < tq), qq, 0.0).astype(BF16)
    m_ref[...] = jnp.full(m_ref.shape, NEG, F32)
    acc_ref[...] = jnp.zeros(acc_ref.shape, F32)
    bufs = ((s0_ref, mb0_ref), (s1_ref, mb1_ref))
    pbufs = ((p0_ref, alpha0_ref), (p1_ref, alpha1_ref))
    causal = (lax.broadcasted_iota(jnp.int32, (tk, 1), 0) // CHUNK
              <= lax.broadcasted_iota(jnp.int32, (1, tk), 1) // CHUNK)

    def scores(j, buf, diagonal=None):
        s_ref, mb_ref = buf
        kb = k_ref[pl.ds(pl.multiple_of(j * tk, tk), tk), :]
        s = _dot(kb, qbd_ref[...])
        if diagonal is None:
            s_ref[:, :2 * tq] = s
            mb_ref[...] = jnp.max(s, axis=0, keepdims=True)
            return
        for g in range(2 * tq // tk):
            cols = slice(g * tk, (g + 1) * tk)
            group = g % (tq // tk)
            piece = s[:, cols]
            if group < diagonal:
                piece = jnp.full_like(piece, NEG)
            elif group == diagonal:
                piece = jnp.where(causal, piece, NEG)
            s_ref[:, cols] = piece
            mb_ref[:, cols] = jnp.max(piece, axis=0, keepdims=True)

    strips = [slice(c * MXU_DIM, (c + 1) * MXU_DIM) for c in range(2 * tq // MXU_DIM)]

    def softmax(buf, pbuf):
        s_ref, mb_ref = buf
        p_ref, alpha_ref = pbuf
        for cols in strips:
            m_old = m_ref[:, cols]
            m_new = jnp.maximum(m_old, mb_ref[:, cols])
            alpha_ref[:, cols] = jnp.exp2(m_old - m_new)
            p_ref[:, cols] = jnp.exp2(s_ref[:, cols] - m_new).astype(BF16)
            m_ref[:, cols] = m_new

    def accumulate(j, pbuf):
        p_ref, alpha_ref = pbuf
        vt = vt_ref[0, j]
        for cols in strips:
            acc_ref[:, cols] = acc_ref[:, cols] * alpha_ref[:, cols] + _dot(vt, p_ref[:, cols])

    assert tq == 2 * tk
    d0 = qi * (tq // tk)
    scores(d0, bufs[0], diagonal=0)
    scores(d0 + 1, bufs[1], diagonal=1)
    softmax(bufs[0], pbufs[0])

    def two_steps(i, carry):
        even_block = jnp.where(i == 0, d0, 2 * i - 2)
        scores(2 * i, bufs[0])
        softmax(bufs[1], pbufs[1])
        accumulate(even_block, pbufs[0])
        scores(2 * i + 1, bufs[1])
        softmax(bufs[0], pbufs[0])
        accumulate(even_block + 1, pbufs[1])
        return carry

    lax.fori_loop(0, qi, two_steps, 0)
    last_even = jnp.where(qi == 0, d0, 2 * qi - 2)
    softmax(bufs[1], pbufs[1])
    accumulate(last_even, pbufs[0])
    accumulate(last_even + 1, pbufs[1])

    a = acc_ref[:HEAD_W, :2 * tq] * (1.0 / acc_ref[HEAD_W:HEAD_W + 1, :2 * tq])
    lam = _lambda(lq1_ref, lk1_ref, lq2_ref, lk2_ref, lam_init)
    o_t = a[:, :tq] - lam * a[:, tq:]
    ms = jnp.mean(o_t * o_t, axis=0, keepdims=True)
    o_t = o_t * lax.rsqrt(ms + EPS) * sg_ref[...] * (1.0 - lam_init)
    o_ref[...] = o_t.T.astype(BF16)


def _attn_prompt(q, kb, vt, lams, sg_col, lam_init):
    seq = q.shape[0]
    tq, tk = ATTN_Q_TILE, ATTN_K_TILE
    assert vt.shape == (N_HEADS, seq // tk, VT_ROWS, tk)
    lam_spec = pl.BlockSpec((1, HEAD_DIM), lambda h, i: (0, 0))
    stat = pltpu.VMEM((1, 2 * tq), F32)
    padded = 2 * tq
    return pl.pallas_call(
        functools.partial(_attn_prompt_kernel, tq=tq, tk=tk, lam_init=lam_init),
        grid=(N_HEADS, seq // tq),
        in_specs=[pl.BlockSpec((tq, HEAD_W), lambda h, i: (i, h)),
                  pl.BlockSpec((seq, HEAD_W), lambda h, i: (0, h)),
                  pl.BlockSpec((1, seq // tk, VT_ROWS, tk), lambda h, i: (h, 0, 0, 0)),
                  lam_spec, lam_spec, lam_spec, lam_spec,
                  pl.BlockSpec((HEAD_W, 1), lambda h, i: (0, 0))],
        out_specs=pl.BlockSpec((tq, HEAD_W), lambda h, i: (i, h)),
        out_shape=jax.ShapeDtypeStruct((seq, N_HEADS * HEAD_W), BF16),
        scratch_shapes=[pltpu.VMEM((HEAD_W, 2 * tq), BF16),
                        pltpu.VMEM((tk, padded), F32),
                        pltpu.VMEM((tk, padded), F32),
                        stat, stat,
                        pltpu.VMEM((tk, padded), BF16),
                        pltpu.VMEM((tk, padded), BF16),
                        stat, stat,
                        stat,
                        pltpu.VMEM((VT_ROWS, padded), F32)],
        compiler_params=_params(2), name="attn_prompt",
    )(q, kb, vt, *lams, sg_col)


def _attn_sample_kernel(q_ref, ck_ref, cv_ref, kn_ref, vn_ref, lq1_ref, lk1_ref,
                        lq2_ref, lk2_ref, sg_ref, o_ref, *, t, lam_init):
    row = lax.broadcasted_iota(jnp.int32, (2 * t, HEAD_W), 0)
    col = lax.broadcasted_iota(jnp.int32, (2 * t, HEAD_W), 1)
    block_diag = (row < t) == (col < HEAD_DIM)
    nt = (((1,), (1,)), ((), ()))
    past = cv_ref.shape[1] // N_HEADS
    lam = _lambda(lq1_ref, lk1_ref, lq2_ref, lk2_ref, lam_init)
    for h in range(N_HEADS):
        lanes = slice(h * HEAD_W, (h + 1) * HEAD_W)
        q = q_ref[:, lanes]
        qq = jnp.concatenate([q, q], axis=0)
        qbd = jnp.where(block_diag, qq, jnp.zeros_like(qq))
        s_c = _dot(qbd, ck_ref[0, lanes, :].astype(BF16))
        s_n = lax.dot_general(qbd, kn_ref[:, lanes].astype(BF16), nt,
                              preferred_element_type=F32)
        m = jnp.maximum(jnp.max(s_c, axis=-1, keepdims=True),
                        jnp.max(s_n, axis=-1, keepdims=True))
        p_c = jnp.exp2(s_c - m)
        p_n = jnp.exp2(s_n - m)
        l = jnp.sum(p_c, axis=-1, keepdims=True) + jnp.sum(p_n, axis=-1, keepdims=True)
        v_c = cv_ref[0, pl.ds(h, past, stride=N_HEADS), :]
        a = (_dot(p_c.astype(BF16), v_c.astype(BF16))
             + _dot(p_n.astype(BF16), vn_ref[:, lanes].astype(BF16))) / l
        o = a[:t] - lam * a[t:]
        o = _rms(o, sg_ref[...]) * (1.0 - lam_init)
        o_ref[:, lanes] = o.astype(BF16)


def _attn_sample(q, cache_kt, cache_v, k_new, v_new, lams, sg_row, lam_init, *, t):
    n_streams, width, past = cache_kt.shape
    new_spec = pl.BlockSpec((t, width), lambda b: (b, 0))
    lam_spec = pl.BlockSpec((1, HEAD_DIM), lambda b: (0, 0))
    return pl.pallas_call(
        functools.partial(_attn_sample_kernel, t=t, lam_init=lam_init),
        grid=(n_streams,),
        in_specs=[new_spec,
                  pl.BlockSpec((1, width, past), lambda b: (b, 0, 0)),
                  pl.BlockSpec((1, past * N_HEADS, HEAD_W), lambda b: (b, 0, 0)),
                  new_spec, new_spec,
                  lam_spec, lam_spec, lam_spec, lam_spec,
                  pl.BlockSpec((1, HEAD_W), lambda b: (0, 0))],
        out_specs=new_spec,
        out_shape=jax.ShapeDtypeStruct((n_streams * t, width), BF16),
        compiler_params=_params(1), name="attn_sample",
    )(q, cache_kt, cache_v, k_new, v_new, *lams, sg_row)


def kernel(x_prompt, x_sample, cache_k, cache_v, state_conv, ffn_norm, ffn_w_gate, ffn_w_up, ffn_w_down, conv_norm, conv_w_pw1, conv_b_pw1, conv_w_dw, conv_b_dw, conv_ln_g, conv_ln_b, conv_w_pw2, conv_b_pw2, kv_norm, w_k, w_v, k_norm, attn_norm, w_q, q_norm, lambda_q1, lambda_k1, lambda_q2, lambda_k2, subln, w_o):
    batch, seq, _ = x_prompt.shape
    dec_batch, dec_seq, _ = x_sample.shape
    assert batch == 1 and seq % ATTN_Q_TILE == 0 and dec_seq % CONV_ROWS == 0
    assert ffn_norm.shape[0] == 2 and conv_norm.shape[0] == 1 and attn_norm.shape[0] == 1
    assert dec_batch * dec_seq == ROW_TILE
    past = cache_k.shape[1]
    qk = N_HEADS * HEAD_W
    row = lambda v: v.reshape(1, -1).astype(F32)
    bf = lambda w: w.astype(BF16)

    ffn_params = (ffn_norm.astype(F32)[:, :, None, :], bf(ffn_w_gate), bf(ffn_w_up), bf(ffn_w_down))
    sub = lambda v: jnp.broadcast_to(v.astype(F32)[..., None, :],
                                     v.shape[:-1] + (SUBLANES, v.shape[-1]))
    conv_w = (row(conv_norm[0]), bf(conv_w_pw1[0]), row(conv_b_pw1[0]), sub(conv_w_dw[0]),
              sub(conv_b_dw[0]), row(conv_ln_g[0]), row(conv_ln_b[0]), bf(conv_w_pw2[0]),
              row(conv_b_pw2[0]))
    kv_w = (row(kv_norm), bf(w_k), bf(w_v), row(jnp.tile(k_norm, qk // HEAD_DIM)))
    q_w = (row(attn_norm[0]), bf(w_q[0]), row(jnp.tile(q_norm[0], qk // HEAD_DIM)))
    lams = (row(lambda_q1[0]), row(lambda_k1[0]), row(lambda_q2[0]), row(lambda_k2[0]))
    lam_init = 0.8 - 0.6 * math.exp(-0.3 * 1)
    sg = subln[0].astype(F32)
    wo = bf(w_o[0])

    xp = x_prompt.reshape(seq, D_MODEL)
    xs = x_sample.reshape(dec_batch * dec_seq, D_MODEL)

    xp, xs = _ffn(xp, xs, ffn_params, 0, 0)
    zero_hist = jnp.zeros((1, CONV_HIST, CONV_C), F32)
    xp, hist_p = _conv(xp, zero_hist, *conv_w, nb=1, t=ROW_TILE)
    xs, hist_s = _conv(xs, state_conv[0], *conv_w, nb=8, t=dec_seq)
    xp, xs = _ffn(xp, xs, ffn_params, 0, 1)
    (k_p, v_p, kb_p, vt_p), (k_s, v_s) = _kv(xp, xs, *kv_w)

    xp, xs = _ffn(xp, xs, ffn_params, 1, 0)
    q_p, q_s = _q(xp, xs, *q_w)
    a_p = _attn_prompt(q_p, kb_p, vt_p, lams, sg.reshape(HEAD_W, 1), lam_init)
    cache_kt = jnp.transpose(cache_k, (0, 2, 3, 4, 1)).reshape(dec_batch, qk, past)
    a_s = _attn_sample(q_s, cache_kt, cache_v.reshape(dec_batch, past * N_HEADS, HEAD_W),
                       k_s, v_s, lams, sg.reshape(1, HEAD_W), lam_init, t=dec_seq)
    yp, ys = _ffn(xp, xs, ffn_params, 1, 1, attn=(a_p, a_s), wo=wo)

    return (yp.reshape(1, seq, D_MODEL),
            ys.reshape(dec_batch, dec_seq, D_MODEL),
            k_p.reshape(1, seq, N_HEADS, 2, HEAD_DIM),
            v_p.reshape(1, seq, N_HEADS, HEAD_W),
            hist_p.reshape(1, 1, CONV_HIST, CONV_C),
            k_s.reshape(dec_batch, dec_seq, N_HEADS, 2, HEAD_DIM),
            v_s.reshape(dec_batch, dec_seq, N_HEADS, HEAD_W),
            hist_s.reshape(1, dec_batch, CONV_HIST, CONV_C))
```

```python
import functools
import math

import jax
import jax.numpy as jnp
from jax import lax
from jax.experimental import pallas as pl
from jax.experimental.pallas import tpu as pltpu

D_MODEL = 1024
D_FF = 2816
CONV_C = 1024
CONV_W = 31
CONV_HIST = CONV_W - 1
N_HEADS = 8
HEAD_DIM = 64
HEAD_W = 2 * HEAD_DIM
VT_ROWS = HEAD_W + 16
CHUNK = 64
EPS = 1e-6
SCALE = 1.0 / math.sqrt(HEAD_DIM)
LOG2E = math.log2(math.e)
NEG = -1e30

F32 = jnp.float32
BF16 = jnp.bfloat16

SUBLANES = 8
LANES = 128
MXU_DIM = 256
ROW_TILE = 512
ATTN_K_TILE = ROW_TILE
ATTN_Q_TILE = 2 * ATTN_K_TILE
CONV_PAD = 32
CONV_ROWS = 32
VMEM_LIMIT_BYTES = 56 * 1024 * 1024


def _params(n_grid_dims):
    return pltpu.CompilerParams(
        dimension_semantics=("arbitrary",) * n_grid_dims,
        vmem_limit_bytes=VMEM_LIMIT_BYTES)


def _const_spec(shape):
    zeros = (0,) * len(shape)
    return pl.BlockSpec(shape, lambda *_: zeros, pipeline_mode=pl.Buffered(1))


def _rows_spec(tm, width):
    return pl.BlockSpec((tm, width), lambda i: (i, 0))


def _dot(a, b):
    return jnp.dot(a, b, preferred_element_type=F32)


def _rms(x, g):
    return x * lax.rsqrt(jnp.mean(x * x, axis=-1, keepdims=True) + EPS) * g


def _group_rms(y, g_tiled):
    r = lax.broadcasted_iota(jnp.int32, (MXU_DIM, MXU_DIM), 0) // HEAD_DIM
    c = lax.broadcasted_iota(jnp.int32, (MXU_DIM, MXU_DIM), 1) // HEAD_DIM
    ones_bd = (r == c).astype(BF16)
    outs = []
    for j in range(y.shape[1] // MXU_DIM):
        ys = y[:, j * MXU_DIM:(j + 1) * MXU_DIM]
        ss = _dot((ys * ys).astype(BF16), ones_bd)
        outs.append(ys * lax.rsqrt(ss * (1.0 / HEAD_DIM) + EPS))
    return jnp.concatenate(outs, axis=1) * g_tiled


def _ffn_body(x, g_ref, wg_ref, wu_ref, wd_ref):
    h = _rms(x, g_ref[...]).astype(BF16)
    gate = _dot(h, wg_ref[...])
    up = _dot(h, wu_ref[...])
    act = (gate * jax.nn.sigmoid(gate) * up).astype(BF16)
    return x + 0.5 * _dot(act, wd_ref[...])


def _ffn_rows(ins, consts, outs):
    (x_ref,), (o_ref,) = ins, outs
    o_ref[...] = _ffn_body(x_ref[...], *consts)


def _proj_ffn_rows(ins, consts, outs):
    (x_ref, a_ref), (o_ref,) = ins, outs
    wo_ref, *ffn_consts = consts
    x = x_ref[...] + _dot(a_ref[...], wo_ref[...])
    o_ref[...] = _ffn_body(x, *ffn_consts)


def _row_out(rows, width, dtype):
    return (jax.ShapeDtypeStruct((rows, width), dtype), (ROW_TILE, width), lambda t: (t, 0))


def _rowwise_call(body, name, prompt_ins, sample_ins, consts, const_specs,
                  prompt_outs, sample_outs):
    n_p = prompt_ins[0].shape[0] // ROW_TILE
    assert all(a.shape[0] == ROW_TILE for a in sample_ins)
    prompt_tile = lambda i: jnp.minimum(i, n_p - 1)
    sample_tile = lambda i: 0
    n_in, n_const, n_po = len(prompt_ins), len(consts), len(prompt_outs)

    def in_spec(a, tile):
        return pl.BlockSpec((ROW_TILE, a.shape[1]), lambda i: (tile(i), 0))

    def out_spec(o, tile):
        _, block, index = o
        return pl.BlockSpec(block, lambda i: index(tile(i)))

    def kernel(*refs):
        p_in, s_in = refs[:n_in], refs[n_in:2 * n_in]
        const_refs = refs[2 * n_in:2 * n_in + n_const]
        out_refs = refs[2 * n_in + n_const:]
        i = pl.program_id(0)

        @pl.when(i < n_p)
        def _():
            body(p_in, const_refs, out_refs[:n_po])

        @pl.when(i >= n_p)
        def _():
            body(s_in, const_refs, out_refs[n_po:])

    outs = pl.pallas_call(
        kernel, grid=(n_p + 1,),
        in_specs=([in_spec(a, prompt_tile) for a in prompt_ins]
                  + [in_spec(a, sample_tile) for a in sample_ins] + list(const_specs)),
        out_specs=([out_spec(o, prompt_tile) for o in prompt_outs]
                   + [out_spec(o, sample_tile) for o in sample_outs]),
        out_shape=[o[0] for o in prompt_outs] + [o[0] for o in sample_outs],
        compiler_params=_params(1), name=name,
    )(*prompt_ins, *sample_ins, *consts)
    return outs[:n_po], outs[n_po:]


def _layer_spec(shape, layer, index):
    tail = (0,) * len(shape)
    return pl.BlockSpec((None, None) + tuple(shape), lambda i: (layer, index) + tail,
                        pipeline_mode=pl.Buffered(1))


def _ffn(xp, xs, ffn_params, layer, index, attn=None, wo=None):
    consts = list(ffn_params)
    const_specs = [_layer_spec((1, D_MODEL), layer, index),
                   _layer_spec((D_MODEL, D_FF), layer, index),
                   _layer_spec((D_MODEL, D_FF), layer, index),
                   _layer_spec((D_FF, D_MODEL), layer, index)]
    p_ins, s_ins, body, name = [xp], [xs], _ffn_rows, "ffn"
    if attn is not None:
        p_ins, s_ins, body, name = [xp, attn[0]], [xs, attn[1]], _proj_ffn_rows, "proj_ffn"
        consts = [wo] + consts
        const_specs = [_const_spec((D_MODEL, D_MODEL))] + const_specs
    (yp,), (ys,) = _rowwise_call(
        body, name, p_ins, s_ins, consts, const_specs,
        [_row_out(xp.shape[0], D_MODEL, F32)], [_row_out(xs.shape[0], D_MODEL, F32)])
    return yp, ys


def _conv_kernel(x_ref, hist_ref, g_ref, w1_ref, b1_ref, wdw_ref, bdw_ref,
                 lng_ref, lnb_ref, w2_ref, b2_ref, o_ref, hist_out_ref,
                 shift_ref, c_ref, *, nb, t, carry):
    i = pl.program_id(0)
    length = CONV_PAD + t
    x = x_ref[...]
    h = _rms(x, g_ref[...]).astype(BF16)
    u = _dot(h, w1_ref[...]) + b1_ref[...]
    glu = u[:, :CONV_C] * jax.nn.sigmoid(u[:, CONV_C:])

    lead = CONV_PAD - CONV_HIST
    if carry:
        @pl.when(i == 0)
        def _():
            shift_ref[0, :, lead:CONV_PAD, :] = hist_ref[...]
    else:
        shift_ref[0, :, lead:CONV_PAD, :] = hist_ref[...]
    shift_ref[0, :, 0:lead, :] = jnp.zeros((nb, lead, CONV_C), F32)
    shift_ref[0, :, CONV_PAD:, :] = glu.reshape(nb, t, CONV_C)
    for s in range(1, SUBLANES):
        shift_ref[s, :, 0:length - SUBLANES, :] = shift_ref[0, :, s:s + length - SUBLANES, :]

    for lt in range(CONV_C // LANES):
        lanes = slice(lt * LANES, (lt + 1) * LANES)
        tap_w = [wdw_ref[w, :, lanes] for w in range(CONV_W)]
        bias = bdw_ref[:, lanes]

        def conv_rows(b, j, lanes=lanes, tap_w=tap_w, bias=bias):
            base = pl.multiple_of(j * CONV_ROWS, CONV_ROWS)
            frames = {}

            def window(off):
                if off not in frames:
                    s = off % SUBLANES
                    frames[off] = shift_ref[s, b, pl.ds(base + (off - s), SUBLANES), lanes]
                return frames[off]

            chains = CONV_ROWS // SUBLANES
            acc = [[bias, None] for _ in range(chains)]
            for off in range(lead, lead + CONV_W + (chains - 1) * SUBLANES):
                for k in range(chains):
                    w = off - lead - k * SUBLANES
                    if 0 <= w < CONV_W:
                        term = window(off) * tap_w[w]
                        part = acc[k][w % 2]
                        acc[k][w % 2] = term if part is None else part + term
            for k in range(chains):
                rows = pl.ds(pl.multiple_of(b * t + base + k * SUBLANES, SUBLANES), SUBLANES)
                c_ref[rows, lanes] = acc[k][0] + acc[k][1]

        def conv_stream(b, carry_val, conv_rows=conv_rows):
            def conv_step(j, c):
                conv_rows(b, j)
                return c
            return lax.fori_loop(0, t // CONV_ROWS, conv_step, carry_val)

        lax.fori_loop(0, nb, conv_stream, 0)

    c = c_ref[...]
    mu = jnp.mean(c, axis=-1, keepdims=True)
    cc = c - mu
    var = jnp.mean(cc * cc, axis=-1, keepdims=True)
    y = cc * lax.rsqrt(var + EPS) * lng_ref[...] + lnb_ref[...]
    y = (y * jax.nn.sigmoid(y)).astype(BF16)
    o_ref[...] = x + _dot(y, w2_ref[...]) + b2_ref[...]

    tail = shift_ref[0, :, length - CONV_HIST:length, :]
    hist_out_ref[...] = tail
    if carry:
        shift_ref[0, :, lead:CONV_PAD, :] = tail


def _conv(x, hist, g, w1, b1, wdw, bdw, lng, lnb, w2, b2, *, nb, t):
    rows = x.shape[0]
    n_streams = hist.shape[0]
    tm = nb * t
    carry = nb == 1 and rows // n_streams > t
    if carry:
        assert n_streams == 1
        hist_map = lambda i: (0, 0, 0)
    else:
        assert rows == n_streams * t
        hist_map = lambda i: (i, 0, 0)
    vec = lambda n: _const_spec((1, n))
    return pl.pallas_call(
        functools.partial(_conv_kernel, nb=nb, t=t, carry=carry),
        grid=(rows // tm,),
        in_specs=[_rows_spec(tm, D_MODEL),
                  pl.BlockSpec((nb, CONV_HIST, CONV_C), hist_map),
                  vec(D_MODEL), _const_spec((D_MODEL, 2 * CONV_C)), vec(2 * CONV_C),
                  _const_spec((CONV_W, SUBLANES, CONV_C)), _const_spec((SUBLANES, CONV_C)),
                  vec(CONV_C), vec(CONV_C),
                  _const_spec((CONV_C, D_MODEL)), vec(D_MODEL)],
        out_specs=[_rows_spec(tm, D_MODEL),
                   pl.BlockSpec((nb, CONV_HIST, CONV_C), hist_map)],
        out_shape=[jax.ShapeDtypeStruct((rows, D_MODEL), F32),
                   jax.ShapeDtypeStruct((n_streams, CONV_HIST, CONV_C), F32)],
        scratch_shapes=[pltpu.VMEM((SUBLANES, nb, CONV_PAD + t, CONV_C), F32),
                        pltpu.VMEM((tm, CONV_C), F32)],
        compiler_params=_params(1), name="conv_module",
    )(x, hist, g, w1, b1, wdw, bdw, lng, lnb, w2, b2)


def _kv_rows(ins, consts, outs):
    (x_ref,), (g_ref, wk_ref, wv_ref, kg_ref) = ins, consts
    h = _rms(x_ref[...], g_ref[...]).astype(BF16)
    k = _group_rms(_dot(h, wk_ref[...]), kg_ref[...])
    v = _dot(h, wv_ref[...])
    outs[0][...] = k
    outs[1][...] = v
    if len(outs) == 4:
        kb_ref, vt_ref = outs[2:]
        kb_ref[...] = k.astype(BF16)
        tm = v.shape[0]
        vt_ref[:, 0, :HEAD_W, :] = v.T.astype(BF16).reshape(N_HEADS, HEAD_W, tm)
        vt_ref[:, 0, HEAD_W:, :] = jnp.ones((N_HEADS, VT_ROWS - HEAD_W, tm), BF16)


def _kv(xp, xs, g, wk, wv, kg):
    qk = N_HEADS * HEAD_W
    n_p = xp.shape[0] // ROW_TILE
    vt_out = (jax.ShapeDtypeStruct((N_HEADS, n_p, VT_ROWS, ROW_TILE), BF16),
              (N_HEADS, 1, VT_ROWS, ROW_TILE), lambda t: (0, t, 0, 0))
    return _rowwise_call(
        _kv_rows, "kv_proj", [xp], [xs], [g, wk, wv, kg],
        [_const_spec((1, D_MODEL)), _const_spec((D_MODEL, qk)), _const_spec((D_MODEL, qk)),
         _const_spec((1, qk))],
        [_row_out(xp.shape[0], qk, F32), _row_out(xp.shape[0], qk, F32),
         _row_out(xp.shape[0], qk, BF16), vt_out],
        [_row_out(xs.shape[0], qk, F32), _row_out(xs.shape[0], qk, F32)])


def _q_rows(ins, consts, outs):
    (x_ref,), (g_ref, wq_ref, qg_ref), (q_ref,) = ins, consts, outs
    h = _rms(x_ref[...], g_ref[...]).astype(BF16)
    q = _group_rms(_dot(h, wq_ref[...]), qg_ref[...]) * (SCALE * LOG2E)
    q_ref[...] = q.astype(BF16)


def _q(xp, xs, g, wq, qg):
    qk = N_HEADS * HEAD_W
    (qp,), (qs,) = _rowwise_call(
        _q_rows, "q_proj", [xp], [xs], [g, wq, qg],
        [_const_spec((1, D_MODEL)), _const_spec((D_MODEL, qk)), _const_spec((1, qk))],
        [_row_out(xp.shape[0], qk, BF16)], [_row_out(xs.shape[0], qk, BF16)])
    return qp, qs


def _lambda(lq1_ref, lk1_ref, lq2_ref, lk2_ref, lam_init):
    s1 = jnp.sum(lq1_ref[...] * lk1_ref[...], axis=-1, keepdims=True)
    s2 = jnp.sum(lq2_ref[...] * lk2_ref[...], axis=-1, keepdims=True)
    return jnp.exp(s1) - jnp.exp(s2) + lam_init


def _attn_prompt_kernel(q_ref, k_ref, vt_ref, lq1_ref, lk1_ref, lq2_ref, lk2_ref,
                        sg_ref, o_ref, qbd_ref, s0_ref, s1_ref, mb0_ref, mb1_ref,
                        p0_ref, p1_ref, alpha0_ref, alpha1_ref,
                        m_ref, acc_ref, *, tq, tk, lam_init):
    qi = pl.program_id(1)
    q_t = q_ref[...].astype(F32).T
    row = lax.broadcasted_iota(jnp.int32, (HEAD_W, 2 * tq), 0)
    col = lax.broadcasted_iota(jnp.int32, (HEAD_W, 2 * tq), 1)
    qq = jnp.concatenate([q_t, q_t], axis=1)
    qbd_ref[...] = jnp.where((row < HEAD_DIM) == (col < tq), qq, 0.0).astype(BF16)
    m_ref[...] = jnp.full(m_ref.shape, NEG, F32)
    acc_ref[...] = jnp.zeros(acc_ref.shape, F32)
    bufs = ((s0_ref, mb0_ref), (s1_ref, mb1_ref))
    pbufs = ((p0_ref, alpha0_ref), (p1_ref, alpha1_ref))
    causal = (lax.broadcasted_iota(jnp.int32, (tk, 1), 0) // CHUNK
              <= lax.broadcasted_iota(jnp.int32, (1, tk), 1) // CHUNK)

    def scores(j, buf, diagonal=None):
        s_ref, mb_ref = buf
        kb = k_ref[pl.ds(pl.multiple_of(j * tk, tk), tk), :]
        s = _dot(kb, qbd_ref[...])
        if diagonal is None:
            s_ref[...] = s
            mb_ref[...] = jnp.max(s, axis=0, keepdims=True)
            return
        for g in range(2 * tq // tk):
            cols = slice(g * tk, (g + 1) * tk)
            group = g % (tq // tk)
            piece = s[:, cols]
            if group < diagonal:
                piece = jnp.full_like(piece, NEG)
            elif group == diagonal:
                piece = jnp.where(causal, piece, NEG)
            s_ref[:, cols] = piece
            mb_ref[:, cols] = jnp.max(piece, axis=0, keepdims=True)

    strips = [slice(c * MXU_DIM, (c + 1) * MXU_DIM) for c in range(2 * tq // MXU_DIM)]

    def softmax(buf, pbuf):
        s_ref, mb_ref = buf
        p_ref, alpha_ref = pbuf
        for cols in strips:
            m_old = m_ref[:, cols]
            m_new = jnp.maximum(m_old, mb_ref[:, cols])
            alpha_ref[:, cols] = jnp.exp2(m_old - m_new)
            p_ref[:, cols] = jnp.exp2(s_ref[:, cols] - m_new).astype(BF16)
            m_ref[:, cols] = m_new

    def accumulate(j, pbuf):
        p_ref, alpha_ref = pbuf
        vt = vt_ref[0, j]
        for cols in strips:
            acc_ref[:, cols] = acc_ref[:, cols] * alpha_ref[:, cols] + _dot(vt, p_ref[:, cols])

    assert tq == 2 * tk
    d0 = qi * (tq // tk)
    scores(d0, bufs[0], diagonal=0)
    scores(d0 + 1, bufs[1], diagonal=1)
    softmax(bufs[0], pbufs[0])

    def two_steps(i, carry):
        even_block = jnp.where(i == 0, d0, 2 * i - 2)
        scores(2 * i, bufs[0])
        softmax(bufs[1], pbufs[1])
        accumulate(even_block, pbufs[0])
        scores(2 * i + 1, bufs[1])
        softmax(bufs[0], pbufs[0])
        accumulate(even_block + 1, pbufs[1])
        return carry

    lax.fori_loop(0, qi, two_steps, 0)
    last_even = jnp.where(qi == 0, d0, 2 * qi - 2)
    softmax(bufs[1], pbufs[1])
    accumulate(last_even, pbufs[0])
    accumulate(last_even + 1, pbufs[1])

    a = acc_ref[:HEAD_W, :2 * tq] * (1.0 / acc_ref[HEAD_W:HEAD_W + 1, :2 * tq])
    lam = _lambda(lq1_ref, lk1_ref, lq2_ref, lk2_ref, lam_init)
    o_t = a[:, :tq] - lam * a[:, tq:]
    ms = jnp.mean(o_t * o_t, axis=0, keepdims=True)
    o_t = o_t * lax.rsqrt(ms + EPS) * sg_ref[...] * (1.0 - lam_init)
    o_ref[...] = o_t.T.astype(BF16)


def _attn_prompt(q, kb, vt, lams, sg_col, lam_init):
    seq = q.shape[0]
    tq, tk = ATTN_Q_TILE, ATTN_K_TILE
    assert vt.shape == (N_HEADS, seq // tk, VT_ROWS, tk)
    lam_spec = pl.BlockSpec((1, HEAD_DIM), lambda h, i: (0, 0))
    stat = pltpu.VMEM((1, 2 * tq), F32)
    width = 2 * tq
    return pl.pallas_call(
        functools.partial(_attn_prompt_kernel, tq=tq, tk=tk, lam_init=lam_init),
        grid=(N_HEADS, seq // tq),
        in_specs=[pl.BlockSpec((tq, HEAD_W), lambda h, i: (i, h)),
                  pl.BlockSpec((seq, HEAD_W), lambda h, i: (0, h)),
                  pl.BlockSpec((1, seq // tk, VT_ROWS, tk), lambda h, i: (h, 0, 0, 0)),
                  lam_spec, lam_spec, lam_spec, lam_spec,
                  pl.BlockSpec((HEAD_W, 1), lambda h, i: (0, 0))],
        out_specs=pl.BlockSpec((tq, HEAD_W), lambda h, i: (i, h)),
        out_shape=jax.ShapeDtypeStruct((seq, N_HEADS * HEAD_W), BF16),
        scratch_shapes=[pltpu.VMEM((HEAD_W, 2 * tq), BF16),
                        pltpu.VMEM((tk, width), F32),
                        pltpu.VMEM((tk, width), F32),
                        stat, stat,
                        pltpu.VMEM((tk, width), BF16),
                        pltpu.VMEM((tk, width), BF16),
                        stat, stat,
                        stat,
                        pltpu.VMEM((VT_ROWS, width), F32)],
        compiler_params=_params(2), name="attn_prompt",
    )(q, kb, vt, *lams, sg_col)


def _attn_sample_kernel(q_ref, ck_ref, cv_ref, kn_ref, vn_ref, lq1_ref, lk1_ref,
                        lq2_ref, lk2_ref, sg_ref, o_ref, *, t, lam_init):
    row = lax.broadcasted_iota(jnp.int32, (2 * t, HEAD_W), 0)
    col = lax.broadcasted_iota(jnp.int32, (2 * t, HEAD_W), 1)
    block_diag = (row < t) == (col < HEAD_DIM)
    nt = (((1,), (1,)), ((), ()))
    past = cv_ref.shape[1] // N_HEADS
    lam = _lambda(lq1_ref, lk1_ref, lq2_ref, lk2_ref, lam_init)
    for h in range(N_HEADS):
        lanes = slice(h * HEAD_W, (h + 1) * HEAD_W)
        q = q_ref[:, lanes]
        qq = jnp.concatenate([q, q], axis=0)
        qbd = jnp.where(block_diag, qq, jnp.zeros_like(qq))
        s_c = _dot(qbd, ck_ref[0, lanes, :].astype(BF16))
        s_n = lax.dot_general(qbd, kn_ref[:, lanes].astype(BF16), nt,
                              preferred_element_type=F32)
        m = jnp.maximum(jnp.max(s_c, axis=-1, keepdims=True),
                        jnp.max(s_n, axis=-1, keepdims=True))
        p_c = jnp.exp2(s_c - m)
        p_n = jnp.exp2(s_n - m)
        l = jnp.sum(p_c, axis=-1, keepdims=True) + jnp.sum(p_n, axis=-1, keepdims=True)
        v_c = cv_ref[0, pl.ds(h, past, stride=N_HEADS), :]
        a = (_dot(p_c.astype(BF16), v_c.astype(BF16))
             + _dot(p_n.astype(BF16), vn_ref[:, lanes].astype(BF16))) / l
        o = a[:t] - lam * a[t:]
        o = _rms(o, sg_ref[...]) * (1.0 - lam_init)
        o_ref[:, lanes] = o.astype(BF16)


def _attn_sample(q, cache_kt, cache_v, k_new, v_new, lams, sg_row, lam_init, *, t):
    n_streams, width, past = cache_kt.shape
    new_spec = pl.BlockSpec((t, width), lambda b: (b, 0))
    lam_spec = pl.BlockSpec((1, HEAD_DIM), lambda b: (0, 0))
    return pl.pallas_call(
        functools.partial(_attn_sample_kernel, t=t, lam_init=lam_init),
        grid=(n_streams,),
        in_specs=[new_spec,
                  pl.BlockSpec((1, width, past), lambda b: (b, 0, 0)),
                  pl.BlockSpec((1, past * N_HEADS, HEAD_W), lambda b: (b, 0, 0)),
                  new_spec, new_spec,
                  lam_spec, lam_spec, lam_spec, lam_spec,
                  pl.BlockSpec((1, HEAD_W), lambda b: (0, 0))],
        out_specs=new_spec,
        out_shape=jax.ShapeDtypeStruct((n_streams * t, width), BF16),
        compiler_params=_params(1), name="attn_sample",
    )(q, cache_kt, cache_v, k_new, v_new, *lams, sg_row)


def kernel(x_prompt, x_sample, cache_k, cache_v, state_conv, ffn_norm, ffn_w_gate, ffn_w_up, ffn_w_down, conv_norm, conv_w_pw1, conv_b_pw1, conv_w_dw, conv_b_dw, conv_ln_g, conv_ln_b, conv_w_pw2, conv_b_pw2, kv_norm, w_k, w_v, k_norm, attn_norm, w_q, q_norm, lambda_q1, lambda_k1, lambda_q2, lambda_k2, subln, w_o):
    batch, seq, _ = x_prompt.shape
    dec_batch, dec_seq, _ = x_sample.shape
    assert batch == 1 and seq % ATTN_Q_TILE == 0 and dec_seq % CONV_ROWS == 0
    assert ffn_norm.shape[0] == 2 and conv_norm.shape[0] == 1 and attn_norm.shape[0] == 1
    assert dec_batch * dec_seq == ROW_TILE
    past = cache_k.shape[1]
    qk = N_HEADS * HEAD_W
    row = lambda v: v.reshape(1, -1).astype(F32)
    bf = lambda w: w.astype(BF16)

    ffn_params = (ffn_norm.astype(F32)[:, :, None, :], bf(ffn_w_gate), bf(ffn_w_up), bf(ffn_w_down))
    sub = lambda v: jnp.broadcast_to(v.astype(F32)[..., None, :],
                                     v.shape[:-1] + (SUBLANES, v.shape[-1]))
    conv_w = (row(conv_norm[0]), bf(conv_w_pw1[0]), row(conv_b_pw1[0]), sub(conv_w_dw[0]),
              sub(conv_b_dw[0]), row(conv_ln_g[0]), row(conv_ln_b[0]), bf(conv_w_pw2[0]),
              row(conv_b_pw2[0]))
    kv_w = (row(kv_norm), bf(w_k), bf(w_v), row(jnp.tile(k_norm, qk // HEAD_DIM)))
    q_w = (row(attn_norm[0]), bf(w_q[0]), row(jnp.tile(q_norm[0], qk // HEAD_DIM)))
    lams = (row(lambda_q1[0]), row(lambda_k1[0]), row(lambda_q2[0]), row(lambda_k2[0]))
    lam_init = 0.8 - 0.6 * math.exp(-0.3 * 1)
    sg = subln[0].astype(F32)
    wo = bf(w_o[0])

    xp = x_prompt.reshape(seq, D_MODEL)
    xs = x_sample.reshape(dec_batch * dec_seq, D_MODEL)

    xp, xs = _ffn(xp, xs, ffn_params, 0, 0)
    zero_hist = jnp.zeros((1, CONV_HIST, CONV_C), F32)
    xp, hist_p = _conv(xp, zero_hist, *conv_w, nb=1, t=ROW_TILE)
    xs, hist_s = _conv(xs, state_conv[0], *conv_w, nb=8, t=dec_seq)
    xp, xs = _ffn(xp, xs, ffn_params, 0, 1)
    (k_p, v_p, kb_p, vt_p), (k_s, v_s) = _kv(xp, xs, *kv_w)

    xp, xs = _ffn(xp, xs, ffn_params, 1, 0)
    q_p, q_s = _q(xp, xs, *q_w)
    a_p = _attn_prompt(q_p, kb_p, vt_p, lams, sg.reshape(HEAD_W, 1), lam_init)
    cache_kt = jnp.transpose(cache_k, (0, 2, 3, 4, 1)).reshape(dec_batch, qk, past)
    a_s = _attn_sample(q_s, cache_kt, cache_v.reshape(dec_batch, past * N_HEADS, HEAD_W),
                       k_s, v_s, lams, sg.reshape(1, HEAD_W), lam_init, t=dec_seq)
    yp, ys = _ffn(xp, xs, ffn_params, 1, 1, attn=(a_p, a_s), wo=wo)

    return (yp.reshape(1, seq, D_MODEL),
            ys.reshape(dec_batch, dec_seq, D_MODEL),
            k_p.reshape(1, seq, N_HEADS, 2, HEAD_DIM),
            v_p.reshape(1, seq, N_HEADS, HEAD_W),
            hist_p.reshape(1, 1, CONV_HIST, CONV_C),
            k_s.reshape(dec_batch, dec_seq, N_HEADS, 2, HEAD_DIM),
            v_s.reshape(dec_batch, dec_seq, N_HEADS, HEAD_W),
            hist_s.reshape(1, dec_batch, CONV_HIST, CONV_C))
```

```python
import functools
import math

import jax
import jax.numpy as jnp
from jax import lax
from jax.experimental import pallas as pl
from jax.experimental.pallas import tpu as pltpu

D_MODEL = 1024
D_FF = 2816
CONV_C = 1024
CONV_W = 31
CONV_HIST = CONV_W - 1
N_HEADS = 8
HEAD_DIM = 64
HEAD_W = 2 * HEAD_DIM
VT_ROWS = HEAD_W + 16
CHUNK = 64
EPS = 1e-6
SCALE = 1.0 / math.sqrt(HEAD_DIM)
LOG2E = math.log2(math.e)
NEG = -1e30

F32 = jnp.float32
BF16 = jnp.bfloat16

SUBLANES = 8
LANES = 128
MXU_DIM = 256
ROW_TILE = 512
ATTN_K_TILE = ROW_TILE
ATTN_Q_TILE = 2 * ATTN_K_TILE
CONV_PAD = 32
CONV_ROWS = 32
VMEM_LIMIT_BYTES = 56 * 1024 * 1024


def _params(n_grid_dims):
    return pltpu.CompilerParams(
        dimension_semantics=("arbitrary",) * n_grid_dims,
        vmem_limit_bytes=VMEM_LIMIT_BYTES)


def _const_spec(shape):
    zeros = (0,) * len(shape)
    return pl.BlockSpec(shape, lambda *_: zeros, pipeline_mode=pl.Buffered(1))


def _rows_spec(tm, width):
    return pl.BlockSpec((tm, width), lambda i: (i, 0))


def _dot(a, b):
    return jnp.dot(a, b, preferred_element_type=F32)


def _rms(x, g):
    return x * lax.rsqrt(jnp.mean(x * x, axis=-1, keepdims=True) + EPS) * g


def _group_rms(y, g_tiled):
    r = lax.broadcasted_iota(jnp.int32, (MXU_DIM, MXU_DIM), 0) // HEAD_DIM
    c = lax.broadcasted_iota(jnp.int32, (MXU_DIM, MXU_DIM), 1) // HEAD_DIM
    ones_bd = (r == c).astype(BF16)
    outs = []
    for j in range(y.shape[1] // MXU_DIM):
        ys = y[:, j * MXU_DIM:(j + 1) * MXU_DIM]
        ss = _dot((ys * ys).astype(BF16), ones_bd)
        outs.append(ys * lax.rsqrt(ss * (1.0 / HEAD_DIM) + EPS))
    return jnp.concatenate(outs, axis=1) * g_tiled


def _ffn_body(x, g_ref, wg_ref, wu_ref, wd_ref):
    h = _rms(x, g_ref[...]).astype(BF16)
    gate = _dot(h, wg_ref[...])
    up = _dot(h, wu_ref[...])
    act = (gate * jax.nn.sigmoid(gate) * up).astype(BF16)
    return x + 0.5 * _dot(act, wd_ref[...])


def _ffn_rows(ins, consts, outs):
    (x_ref,), (o_ref,) = ins, outs
    o_ref[...] = _ffn_body(x_ref[...], *consts)


def _proj_ffn_rows(ins, consts, outs):
    (x_ref, a_ref), (o_ref,) = ins, outs
    wo_ref, *ffn_consts = consts
    x = x_ref[...] + _dot(a_ref[...], wo_ref[...])
    o_ref[...] = _ffn_body(x, *ffn_consts)


def _row_out(rows, width, dtype):
    return (jax.ShapeDtypeStruct((rows, width), dtype), (ROW_TILE, width), lambda t: (t, 0))


def _rowwise_call(body, name, prompt_ins, sample_ins, consts, const_specs,
                  prompt_outs, sample_outs):
    n_p = prompt_ins[0].shape[0] // ROW_TILE
    assert all(a.shape[0] == ROW_TILE for a in sample_ins)
    prompt_tile = lambda i: jnp.minimum(i, n_p - 1)
    sample_tile = lambda i: 0
    n_in, n_const, n_po = len(prompt_ins), len(consts), len(prompt_outs)

    def in_spec(a, tile):
        return pl.BlockSpec((ROW_TILE, a.shape[1]), lambda i: (tile(i), 0))

    def out_spec(o, tile):
        _, block, index = o
        return pl.BlockSpec(block, lambda i: index(tile(i)))

    def kernel(*refs):
        p_in, s_in = refs[:n_in], refs[n_in:2 * n_in]
        const_refs = refs[2 * n_in:2 * n_in + n_const]
        out_refs = refs[2 * n_in + n_const:]
        i = pl.program_id(0)

        @pl.when(i < n_p)
        def _():
            body(p_in, const_refs, out_refs[:n_po])

        @pl.when(i >= n_p)
        def _():
            body(s_in, const_refs, out_refs[n_po:])

    outs = pl.pallas_call(
        kernel, grid=(n_p + 1,),
        in_specs=([in_spec(a, prompt_tile) for a in prompt_ins]
                  + [in_spec(a, sample_tile) for a in sample_ins] + list(const_specs)),
        out_specs=([out_spec(o, prompt_tile) for o in prompt_outs]
                   + [out_spec(o, sample_tile) for o in sample_outs]),
        out_shape=[o[0] for o in prompt_outs] + [o[0] for o in sample_outs],
        compiler_params=_params(1), name=name,
    )(*prompt_ins, *sample_ins, *consts)
    return outs[:n_po], outs[n_po:]


def _layer_spec(shape, layer, index):
    tail = (0,) * len(shape)
    return pl.BlockSpec((None, None) + tuple(shape), lambda i: (layer, index) + tail,
                        pipeline_mode=pl.Buffered(1))


def _ffn(xp, xs, ffn_params, layer, index, attn=None, wo=None):
    consts = list(ffn_params)
    const_specs = [_layer_spec((1, D_MODEL), layer, index),
                   _layer_spec((D_MODEL, D_FF), layer, index),
                   _layer_spec((D_MODEL, D_FF), layer, index),
                   _layer_spec((D_FF, D_MODEL), layer, index)]
    p_ins, s_ins, body, name = [xp], [xs], _ffn_rows, "ffn"
    if attn is not None:
        p_ins, s_ins, body, name = [xp, attn[0]], [xs, attn[1]], _proj_ffn_rows, "proj_ffn"
        consts = [wo] + consts
        const_specs = [_const_spec((D_MODEL, D_MODEL))] + const_specs
    (yp,), (ys,) = _rowwise_call(
        body, name, p_ins, s_ins, consts, const_specs,
        [_row_out(xp.shape[0], D_MODEL, F32)], [_row_out(xs.shape[0], D_MODEL, F32)])
    return yp, ys


def _conv_kernel(x_ref, hist_ref, g_ref, w1_ref, b1_ref, wdw_ref, bdw_ref,
                 lng_ref, lnb_ref, w2_ref, b2_ref, o_ref, hist_out_ref,
                 shift_ref, c_ref, *, nb, t, carry):
    i = pl.program_id(0)
    length = CONV_PAD + t
    x = x_ref[...]
    h = _rms(x, g_ref[...]).astype(BF16)
    u = _dot(h, w1_ref[...]) + b1_ref[...]
    glu = u[:, :CONV_C] * jax.nn.sigmoid(u[:, CONV_C:])

    lead = CONV_PAD - CONV_HIST
    if carry:
        @pl.when(i == 0)
        def _():
            shift_ref[0, :, lead:CONV_PAD, :] = hist_ref[...]
    else:
        shift_ref[0, :, lead:CONV_PAD, :] = hist_ref[...]
    shift_ref[0, :, 0:lead, :] = jnp.zeros((nb, lead, CONV_C), F32)
    shift_ref[0, :, CONV_PAD:, :] = glu.reshape(nb, t, CONV_C)
    for s in range(1, SUBLANES):
        shift_ref[s, :, 0:length - SUBLANES, :] = shift_ref[0, :, s:s + length - SUBLANES, :]

    for lt in range(CONV_C // LANES):
        lanes = slice(lt * LANES, (lt + 1) * LANES)
        tap_w = [wdw_ref[w, :, lanes] for w in range(CONV_W)]
        bias = bdw_ref[:, lanes]

        def conv_rows(b, j, lanes=lanes, tap_w=tap_w, bias=bias):
            base = pl.multiple_of(j * CONV_ROWS, CONV_ROWS)
            frames = {}

            def window(off):
                if off not in frames:
                    s = off % SUBLANES
                    frames[off] = shift_ref[s, b, pl.ds(base + (off - s), SUBLANES), lanes]
                return frames[off]

            chains = CONV_ROWS // SUBLANES
            acc = [[bias, None] for _ in range(chains)]
            for off in range(lead, lead + CONV_W + (chains - 1) * SUBLANES):
                for k in range(chains):
                    w = off - lead - k * SUBLANES
                    if 0 <= w < CONV_W:
                        term = window(off) * tap_w[w]
                        part = acc[k][w % 2]
                        acc[k][w % 2] = term if part is None else part + term
            for k in range(chains):
                rows = pl.ds(pl.multiple_of(b * t + base + k * SUBLANES, SUBLANES), SUBLANES)
                c_ref[rows, lanes] = acc[k][0] + acc[k][1]

        def conv_stream(b, carry_val, conv_rows=conv_rows):
            def conv_step(j, c):
                conv_rows(b, j)
                return c
            return lax.fori_loop(0, t // CONV_ROWS, conv_step, carry_val)

        lax.fori_loop(0, nb, conv_stream, 0)

    c = c_ref[...]
    mu = jnp.mean(c, axis=-1, keepdims=True)
    cc = c - mu
    var = jnp.mean(cc * cc, axis=-1, keepdims=True)
    y = cc * lax.rsqrt(var + EPS) * lng_ref[...] + lnb_ref[...]
    y = (y * jax.nn.sigmoid(y)).astype(BF16)
    o_ref[...] = x + _dot(y, w2_ref[...]) + b2_ref[...]

    tail = shift_ref[0, :, length - CONV_HIST:length, :]
    hist_out_ref[...] = tail
    if carry:
        shift_ref[0, :, lead:CONV_PAD, :] = tail


def _conv(x, hist, g, w1, b1, wdw, bdw, lng, lnb, w2, b2, *, nb, t):
    rows = x.shape[0]
    n_streams = hist.shape[0]
    tm = nb * t
    carry = nb == 1 and rows // n_streams > t
    if carry:
        assert n_streams == 1
        hist_map = lambda i: (0, 0, 0)
    else:
        assert rows == n_streams * t
        hist_map = lambda i: (i, 0, 0)
    vec = lambda n: _const_spec((1, n))
    return pl.pallas_call(
        functools.partial(_conv_kernel, nb=nb, t=t, carry=carry),
        grid=(rows // tm,),
        in_specs=[_rows_spec(tm, D_MODEL),
                  pl.BlockSpec((nb, CONV_HIST, CONV_C), hist_map),
                  vec(D_MODEL), _const_spec((D_MODEL, 2 * CONV_C)), vec(2 * CONV_C),
                  _const_spec((CONV_W, SUBLANES, CONV_C)), _const_spec((SUBLANES, CONV_C)),
                  vec(CONV_C), vec(CONV_C),
                  _const_spec((CONV_C, D_MODEL)), vec(D_MODEL)],
        out_specs=[_rows_spec(tm, D_MODEL),
                   pl.BlockSpec((nb, CONV_HIST, CONV_C), hist_map)],
        out_shape=[jax.ShapeDtypeStruct((rows, D_MODEL), F32),
                   jax.ShapeDtypeStruct((n_streams, CONV_HIST, CONV_C), F32)],
        scratch_shapes=[pltpu.VMEM((SUBLANES, nb, CONV_PAD + t, CONV_C), F32),
                        pltpu.VMEM((tm, CONV_C), F32)],
        compiler_params=_params(1), name="conv_module",
    )(x, hist, g, w1, b1, wdw, bdw, lng, lnb, w2, b2)


def _kv_rows(ins, consts, outs):
    (x_ref,), (g_ref, wk_ref, wv_ref, kg_ref) = ins, consts
    h = _rms(x_ref[...], g_ref[...]).astype(BF16)
    k = _group_rms(_dot(h, wk_ref[...]), kg_ref[...])
    v = _dot(h, wv_ref[...])
    outs[0][...] = k
    outs[1][...] = v
    if len(outs) == 4:
        kb_ref, vt_ref = outs[2:]
        kb_ref[...] = k.astype(BF16)
        tm = v.shape[0]
        vt_ref[:, 0, :HEAD_W, :] = v.T.astype(BF16).reshape(N_HEADS, HEAD_W, tm)
        vt_ref[:, 0, HEAD_W:, :] = jnp.ones((N_HEADS, VT_ROWS - HEAD_W, tm), BF16)


def _kv(xp, xs, g, wk, wv, kg):
    qk = N_HEADS * HEAD_W
    n_p = xp.shape[0] // ROW_TILE
    vt_out = (jax.ShapeDtypeStruct((N_HEADS, n_p, VT_ROWS, ROW_TILE), BF16),
              (N_HEADS, 1, VT_ROWS, ROW_TILE), lambda t: (0, t, 0, 0))
    return _rowwise_call(
        _kv_rows, "kv_proj", [xp], [xs], [g, wk, wv, kg],
        [_const_spec((1, D_MODEL)), _const_spec((D_MODEL, qk)), _const_spec((D_MODEL, qk)),
         _const_spec((1, qk))],
        [_row_out(xp.shape[0], qk, F32), _row_out(xp.shape[0], qk, F32),
         _row_out(xp.shape[0], qk, BF16), vt_out],
        [_row_out(xs.shape[0], qk, F32), _row_out(xs.shape[0], qk, F32)])


def _ffn_q_rows(ins, consts, outs):
    (x_ref,), (o_ref, q_ref) = ins, outs
    *ffn_consts, g_ref, wq_ref, qg_ref = consts
    x = _ffn_body(x_ref[...], *ffn_consts)
    o_ref[...] = x
    h = _rms(x, g_ref[...]).astype(BF16)
    q = _group_rms(_dot(h, wq_ref[...]), qg_ref[...]) * (SCALE * LOG2E)
    q_ref[...] = q.astype(BF16)


def _ffn_q(xp, xs, ffn_params, layer, index, g, wq, qg):
    qk = N_HEADS * HEAD_W
    const_specs = [_layer_spec((1, D_MODEL), layer, index),
                   _layer_spec((D_MODEL, D_FF), layer, index),
                   _layer_spec((D_MODEL, D_FF), layer, index),
                   _layer_spec((D_FF, D_MODEL), layer, index),
                   _const_spec((1, D_MODEL)), _const_spec((D_MODEL, qk)), _const_spec((1, qk))]
    (yp, qp), (ys, qs) = _rowwise_call(
        _ffn_q_rows, "ffn_q_proj", [xp], [xs], list(ffn_params) + [g, wq, qg], const_specs,
        [_row_out(xp.shape[0], D_MODEL, F32), _row_out(xp.shape[0], qk, BF16)],
        [_row_out(xs.shape[0], D_MODEL, F32), _row_out(xs.shape[0], qk, BF16)])
    return yp, ys, qp, qs


def _lambda(lq1_ref, lk1_ref, lq2_ref, lk2_ref, lam_init):
    s1 = jnp.sum(lq1_ref[...] * lk1_ref[...], axis=-1, keepdims=True)
    s2 = jnp.sum(lq2_ref[...] * lk2_ref[...], axis=-1, keepdims=True)
    return jnp.exp(s1) - jnp.exp(s2) + lam_init


def _attn_prompt_kernel(q_ref, k_ref, vt_ref, lq1_ref, lk1_ref, lq2_ref, lk2_ref,
                        sg_ref, o_ref, qbd_ref, s0_ref, s1_ref, mb0_ref, mb1_ref,
                        p0_ref, p1_ref, alpha0_ref, alpha1_ref,
                        m_ref, acc_ref, *, tq, tk, lam_init):
    qi = pl.program_id(1)
    q_t = q_ref[...].astype(F32).T
    row = lax.broadcasted_iota(jnp.int32, (HEAD_W, 2 * tq), 0)
    col = lax.broadcasted_iota(jnp.int32, (HEAD_W, 2 * tq), 1)
    qq = jnp.concatenate([q_t, q_t], axis=1)
    qbd_ref[...] = jnp.where((row < HEAD_DIM) == (col < tq), qq, 0.0).astype(BF16)
    m_ref[...] = jnp.full(m_ref.shape, NEG, F32)
    acc_ref[...] = jnp.zeros(acc_ref.shape, F32)
    bufs = ((s0_ref, mb0_ref), (s1_ref, mb1_ref))
    pbufs = ((p0_ref, alpha0_ref), (p1_ref, alpha1_ref))
    causal = (lax.broadcasted_iota(jnp.int32, (tk, 1), 0) // CHUNK
              <= lax.broadcasted_iota(jnp.int32, (1, tk), 1) // CHUNK)

    def scores(j, buf, diagonal=None):
        s_ref, mb_ref = buf
        kb = k_ref[pl.ds(pl.multiple_of(j * tk, tk), tk), :]
        s = _dot(kb, qbd_ref[...])
        if diagonal is None:
            s_ref[...] = s
            mb_ref[...] = jnp.max(s, axis=0, keepdims=True)
            return
        for g in range(2 * tq // tk):
            cols = slice(g * tk, (g + 1) * tk)
            group = g % (tq // tk)
            piece = s[:, cols]
            if group < diagonal:
                piece = jnp.full_like(piece, NEG)
            elif group == diagonal:
                piece = jnp.where(causal, piece, NEG)
            s_ref[:, cols] = piece
            mb_ref[:, cols] = jnp.max(piece, axis=0, keepdims=True)

    strips = [slice(c * MXU_DIM, (c + 1) * MXU_DIM) for c in range(2 * tq // MXU_DIM)]

    def softmax(buf, pbuf):
        s_ref, mb_ref = buf
        p_ref, alpha_ref = pbuf
        for cols in strips:
            m_old = m_ref[:, cols]
            m_new = jnp.maximum(m_old, mb_ref[:, cols])
            alpha_ref[:, cols] = jnp.exp2(m_old - m_new)
            p_ref[:, cols] = jnp.exp2(s_ref[:, cols] - m_new).astype(BF16)
            m_ref[:, cols] = m_new

    def accumulate(j, pbuf):
        p_ref, alpha_ref = pbuf
        vt = vt_ref[0, j]
        for cols in strips:
            acc_ref[:, cols] = acc_ref[:, cols] * alpha_ref[:, cols] + _dot(vt, p_ref[:, cols])

    assert tq == 2 * tk
    d0 = qi * (tq // tk)
    scores(d0, bufs[0], diagonal=0)
    scores(d0 + 1, bufs[1], diagonal=1)
    softmax(bufs[0], pbufs[0])

    def two_steps(i, carry):
        even_block = jnp.where(i == 0, d0, 2 * i - 2)
        scores(2 * i, bufs[0])
        softmax(bufs[1], pbufs[1])
        accumulate(even_block, pbufs[0])
        scores(2 * i + 1, bufs[1])
        softmax(bufs[0], pbufs[0])
        accumulate(even_block + 1, pbufs[1])
        return carry

    lax.fori_loop(0, qi, two_steps, 0)
    last_even = jnp.where(qi == 0, d0, 2 * qi - 2)
    softmax(bufs[1], pbufs[1])
    accumulate(last_even, pbufs[0])
    accumulate(last_even + 1, pbufs[1])

    a = acc_ref[:HEAD_W, :2 * tq] * (1.0 / acc_ref[HEAD_W:HEAD_W + 1, :2 * tq])
    lam = _lambda(lq1_ref, lk1_ref, lq2_ref, lk2_ref, lam_init)
    o_t = a[:, :tq] - lam * a[:, tq:]
    ms = jnp.mean(o_t * o_t, axis=0, keepdims=True)
    o_t = o_t * lax.rsqrt(ms + EPS) * sg_ref[...] * (1.0 - lam_init)
    o_ref[...] = o_t.T.astype(BF16)


def _attn_prompt(q, kb, vt, lams, sg_col, lam_init):
    seq = q.shape[0]
    tq, tk = ATTN_Q_TILE, ATTN_K_TILE
    assert vt.shape == (N_HEADS, seq // tk, VT_ROWS, tk)
    lam_spec = pl.BlockSpec((1, HEAD_DIM), lambda h, i: (0, 0))
    stat = pltpu.VMEM((1, 2 * tq), F32)
    width = 2 * tq
    return pl.pallas_call(
        functools.partial(_attn_prompt_kernel, tq=tq, tk=tk, lam_init=lam_init),
        grid=(N_HEADS, seq // tq),
        in_specs=[pl.BlockSpec((tq, HEAD_W), lambda h, i: (i, h)),
                  pl.BlockSpec((seq, HEAD_W), lambda h, i: (0, h)),
                  pl.BlockSpec((1, seq // tk, VT_ROWS, tk), lambda h, i: (h, 0, 0, 0)),
                  lam_spec, lam_spec, lam_spec, lam_spec,
                  pl.BlockSpec((HEAD_W, 1), lambda h, i: (0, 0))],
        out_specs=pl.BlockSpec((tq, HEAD_W), lambda h, i: (i, h)),
        out_shape=jax.ShapeDtypeStruct((seq, N_HEADS * HEAD_W), BF16),
        scratch_shapes=[pltpu.VMEM((HEAD_W, 2 * tq), BF16),
                        pltpu.VMEM((tk, width), F32),
                        pltpu.VMEM((tk, width), F32),
                        stat, stat,
                        pltpu.VMEM((tk, width), BF16),
                        pltpu.VMEM((tk, width), BF16),
                        stat, stat,
                        stat,
                        pltpu.VMEM((VT_ROWS, width), F32)],
        compiler_params=_params(2), name="attn_prompt",
    )(q, kb, vt, *lams, sg_col)


def _attn_sample_kernel(q_ref, ck_ref, cv_ref, kn_ref, vn_ref, lq1_ref, lk1_ref,
                        lq2_ref, lk2_ref, sg_ref, o_ref, *, t, lam_init):
    row = lax.broadcasted_iota(jnp.int32, (2 * t, HEAD_W), 0)
    col = lax.broadcasted_iota(jnp.int32, (2 * t, HEAD_W), 1)
    block_diag = (row < t) == (col < HEAD_DIM)
    nt = (((1,), (1,)), ((), ()))
    past = cv_ref.shape[1] // N_HEADS
    lam = _lambda(lq1_ref, lk1_ref, lq2_ref, lk2_ref, lam_init)
    for h in range(N_HEADS):
        lanes = slice(h * HEAD_W, (h + 1) * HEAD_W)
        q = q_ref[:, lanes]
        qq = jnp.concatenate([q, q], axis=0)
        qbd = jnp.where(block_diag, qq, jnp.zeros_like(qq))
        s_c = _dot(qbd, ck_ref[0, lanes, :].astype(BF16))
        s_n = lax.dot_general(qbd, kn_ref[:, lanes].astype(BF16), nt,
                              preferred_element_type=F32)
        m = jnp.maximum(jnp.max(s_c, axis=-1, keepdims=True),
                        jnp.max(s_n, axis=-1, keepdims=True))
        p_c = jnp.exp2(s_c - m)
        p_n = jnp.exp2(s_n - m)
        l = jnp.sum(p_c, axis=-1, keepdims=True) + jnp.sum(p_n, axis=-1, keepdims=True)
        v_c = cv_ref[0, pl.ds(h, past, stride=N_HEADS), :]
        a = (_dot(p_c.astype(BF16), v_c.astype(BF16))
             + _dot(p_n.astype(BF16), vn_ref[:, lanes].astype(BF16))) / l
        o = a[:t] - lam * a[t:]
        o = _rms(o, sg_ref[...]) * (1.0 - lam_init)
        o_ref[:, lanes] = o.astype(BF16)


def _attn_sample(q, cache_kt, cache_v, k_new, v_new, lams, sg_row, lam_init, *, t):
    n_streams, width, past = cache_kt.shape
    new_spec = pl.BlockSpec((t, width), lambda b: (b, 0))
    lam_spec = pl.BlockSpec((1, HEAD_DIM), lambda b: (0, 0))
    return pl.pallas_call(
        functools.partial(_attn_sample_kernel, t=t, lam_init=lam_init),
        grid=(n_streams,),
        in_specs=[new_spec,
                  pl.BlockSpec((1, width, past), lambda b: (b, 0, 0)),
                  pl.BlockSpec((1, past * N_HEADS, HEAD_W), lambda b: (b, 0, 0)),
                  new_spec, new_spec,
                  lam_spec, lam_spec, lam_spec, lam_spec,
                  pl.BlockSpec((1, HEAD_W), lambda b: (0, 0))],
        out_specs=new_spec,
        out_shape=jax.ShapeDtypeStruct((n_streams * t, width), BF16),
        compiler_params=_params(1), name="attn_sample",
    )(q, cache_kt, cache_v, k_new, v_new, *lams, sg_row)


def kernel(x_prompt, x_sample, cache_k, cache_v, state_conv, ffn_norm, ffn_w_gate, ffn_w_up, ffn_w_down, conv_norm, conv_w_pw1, conv_b_pw1, conv_w_dw, conv_b_dw, conv_ln_g, conv_ln_b, conv_w_pw2, conv_b_pw2, kv_norm, w_k, w_v, k_norm, attn_norm, w_q, q_norm, lambda_q1, lambda_k1, lambda_q2, lambda_k2, subln, w_o):
    batch, seq, _ = x_prompt.shape
    dec_batch, dec_seq, _ = x_sample.shape
    assert batch == 1 and seq % ATTN_Q_TILE == 0 and dec_seq % CONV_ROWS == 0
    assert ffn_norm.shape[0] == 2 and conv_norm.shape[0] == 1 and attn_norm.shape[0] == 1
    assert dec_batch * dec_seq == ROW_TILE
    past = cache_k.shape[1]
    qk = N_HEADS * HEAD_W
    row = lambda v: v.reshape(1, -1).astype(F32)
    bf = lambda w: w.astype(BF16)

    ffn_params = (ffn_norm.astype(F32)[:, :, None, :], bf(ffn_w_gate), bf(ffn_w_up), bf(ffn_w_down))
    sub = lambda v: jnp.broadcast_to(v.astype(F32)[..., None, :],
                                     v.shape[:-1] + (SUBLANES, v.shape[-1]))
    conv_w = (row(conv_norm[0]), bf(conv_w_pw1[0]), row(conv_b_pw1[0]), sub(conv_w_dw[0]),
              sub(conv_b_dw[0]), row(conv_ln_g[0]), row(conv_ln_b[0]), bf(conv_w_pw2[0]),
              row(conv_b_pw2[0]))
    kv_w = (row(kv_norm), bf(w_k), bf(w_v), row(jnp.tile(k_norm, qk // HEAD_DIM)))
    q_w = (row(attn_norm[0]), bf(w_q[0]), row(jnp.tile(q_norm[0], qk // HEAD_DIM)))
    lams = (row(lambda_q1[0]), row(lambda_k1[0]), row(lambda_q2[0]), row(lambda_k2[0]))
    lam_init = 0.8 - 0.6 * math.exp(-0.3 * 1)
    sg = subln[0].astype(F32)
    wo = bf(w_o[0])

    xp = x_prompt.reshape(seq, D_MODEL)
    xs = x_sample.reshape(dec_batch * dec_seq, D_MODEL)

    xp, xs = _ffn(xp, xs, ffn_params, 0, 0)
    zero_hist = jnp.zeros((1, CONV_HIST, CONV_C), F32)
    xp, hist_p = _conv(xp, zero_hist, *conv_w, nb=1, t=ROW_TILE)
    xs, hist_s = _conv(xs, state_conv[0], *conv_w, nb=8, t=dec_seq)
    xp, xs = _ffn(xp, xs, ffn_params, 0, 1)
    (k_p, v_p, kb_p, vt_p), (k_s, v_s) = _kv(xp, xs, *kv_w)

    xp, xs, q_p, q_s = _ffn_q(xp, xs, ffn_params, 1, 0, *q_w)
    a_p = _attn_prompt(q_p, kb_p, vt_p, lams, sg.reshape(HEAD_W, 1), lam_init)
    cache_kt = jnp.transpose(cache_k, (0, 2, 3, 4, 1)).reshape(dec_batch, qk, past)
    a_s = _attn_sample(q_s, cache_kt, cache_v.reshape(dec_batch, past * N_HEADS, HEAD_W),
                       k_s, v_s, lams, sg.reshape(1, HEAD_W), lam_init, t=dec_seq)
    yp, ys = _ffn(xp, xs, ffn_params, 1, 1, attn=(a_p, a_s), wo=wo)

    return (yp.reshape(1, seq, D_MODEL),
            ys.reshape(dec_batch, dec_seq, D_MODEL),
            k_p.reshape(1, seq, N_HEADS, 2, HEAD_DIM),
            v_p.reshape(1, seq, N_HEADS, HEAD_W),
            hist_p.reshape(1, 1, CONV_HIST, CONV_C),
            k_s.reshape(dec_batch, dec_seq, N_HEADS, 2, HEAD_DIM),
            v_s.reshape(dec_batch, dec_seq, N_HEADS, HEAD_W),
            hist_s.reshape(1, dec_batch, CONV_HIST, CONV_C))
```

```python
import functools
import math

import jax
import jax.numpy as jnp
from jax import lax
from jax.experimental import pallas as pl
from jax.experimental.pallas import tpu as pltpu

D_MODEL = 1024
D_FF = 2816
CONV_C = 1024
CONV_W = 31
CONV_HIST = CONV_W - 1
N_HEADS = 8
HEAD_DIM = 64
HEAD_W = 2 * HEAD_DIM
VT_ROWS = HEAD_W + 16
CHUNK = 64
EPS = 1e-6
SCALE = 1.0 / math.sqrt(HEAD_DIM)
LOG2E = math.log2(math.e)
NEG = -1e30

F32 = jnp.float32
BF16 = jnp.bfloat16

SUBLANES = 8
LANES = 128
MXU_DIM = 256
ROW_TILE = 512
ATTN_K_TILE = ROW_TILE
ATTN_Q_TILE = 2 * ATTN_K_TILE
CONV_PAD = 32
CONV_ROWS = 32
VMEM_LIMIT_BYTES = 56 * 1024 * 1024


def _params(n_grid_dims):
    return pltpu.CompilerParams(
        dimension_semantics=("arbitrary",) * n_grid_dims,
        vmem_limit_bytes=VMEM_LIMIT_BYTES)


def _const_spec(shape):
    zeros = (0,) * len(shape)
    return pl.BlockSpec(shape, lambda *_: zeros, pipeline_mode=pl.Buffered(1))


def _rows_spec(tm, width):
    return pl.BlockSpec((tm, width), lambda i: (i, 0))


def _dot(a, b):
    return jnp.dot(a, b, preferred_element_type=F32)


def _rms(x, g):
    return x * lax.rsqrt(jnp.mean(x * x, axis=-1, keepdims=True) + EPS) * g


def _group_rms(y, g_tiled):
    r = lax.broadcasted_iota(jnp.int32, (MXU_DIM, MXU_DIM), 0) // HEAD_DIM
    c = lax.broadcasted_iota(jnp.int32, (MXU_DIM, MXU_DIM), 1) // HEAD_DIM
    ones_bd = (r == c).astype(BF16)
    outs = []
    for j in range(y.shape[1] // MXU_DIM):
        ys = y[:, j * MXU_DIM:(j + 1) * MXU_DIM]
        ss = _dot((ys * ys).astype(BF16), ones_bd)
        outs.append(ys * lax.rsqrt(ss * (1.0 / HEAD_DIM) + EPS))
    return jnp.concatenate(outs, axis=1) * g_tiled


def _ffn_body(x, g_ref, wg_ref, wu_ref, wd_ref):
    h = _rms(x, g_ref[...]).astype(BF16)
    gate = _dot(h, wg_ref[...])
    up = _dot(h, wu_ref[...])
    act = (gate * jax.nn.sigmoid(gate) * up).astype(BF16)
    return x + 0.5 * _dot(act, wd_ref[...])


def _ffn_rows(ins, consts, outs):
    (x_ref,), (o_ref,) = ins, outs
    o_ref[...] = _ffn_body(x_ref[...], *consts)


def _proj_ffn_rows(ins, consts, outs):
    (x_ref, a_ref), (o_ref,) = ins, outs
    wo_ref, *ffn_consts = consts
    x = x_ref[...] + _dot(a_ref[...], wo_ref[...])
    o_ref[...] = _ffn_body(x, *ffn_consts)


def _row_out(rows, width, dtype):
    return (jax.ShapeDtypeStruct((rows, width), dtype), (ROW_TILE, width), lambda t: (t, 0))


def _rowwise_call(body, name, prompt_ins, sample_ins, consts, const_specs,
                  prompt_outs, sample_outs):
    n_p = prompt_ins[0].shape[0] // ROW_TILE
    assert all(a.shape[0] == ROW_TILE for a in sample_ins)
    prompt_tile = lambda i: jnp.minimum(i, n_p - 1)
    sample_tile = lambda i: 0
    n_in, n_const, n_po = len(prompt_ins), len(consts), len(prompt_outs)

    def in_spec(a, tile):
        return pl.BlockSpec((ROW_TILE, a.shape[1]), lambda i: (tile(i), 0))

    def out_spec(o, tile):
        _, block, index = o
        return pl.BlockSpec(block, lambda i: index(tile(i)))

    def kernel(*refs):
        p_in, s_in = refs[:n_in], refs[n_in:2 * n_in]
        const_refs = refs[2 * n_in:2 * n_in + n_const]
        out_refs = refs[2 * n_in + n_const:]
        i = pl.program_id(0)

        @pl.when(i < n_p)
        def _():
            body(p_in, const_refs, out_refs[:n_po])

        @pl.when(i >= n_p)
        def _():
            body(s_in, const_refs, out_refs[n_po:])

    outs = pl.pallas_call(
        kernel, grid=(n_p + 1,),
        in_specs=([in_spec(a, prompt_tile) for a in prompt_ins]
                  + [in_spec(a, sample_tile) for a in sample_ins] + list(const_specs)),
        out_specs=([out_spec(o, prompt_tile) for o in prompt_outs]
                   + [out_spec(o, sample_tile) for o in sample_outs]),
        out_shape=[o[0] for o in prompt_outs] + [o[0] for o in sample_outs],
        compiler_params=_params(1), name=name,
    )(*prompt_ins, *sample_ins, *consts)
    return outs[:n_po], outs[n_po:]


def _layer_spec(shape, layer, index):
    tail = (0,) * len(shape)
    return pl.BlockSpec((None, None) + tuple(shape), lambda i: (layer, index) + tail,
                        pipeline_mode=pl.Buffered(1))


def _ffn(xp, xs, ffn_params, layer, index, attn=None, wo=None):
    consts = list(ffn_params)
    const_specs = [_layer_spec((1, D_MODEL), layer, index),
                   _layer_spec((D_MODEL, D_FF), layer, index),
                   _layer_spec((D_MODEL, D_FF), layer, index),
                   _layer_spec((D_FF, D_MODEL), layer, index)]
    p_ins, s_ins, body, name = [xp], [xs], _ffn_rows, "ffn"
    if attn is not None:
        p_ins, s_ins, body, name = [xp, attn[0]], [xs, attn[1]], _proj_ffn_rows, "proj_ffn"
        consts = [wo] + consts
        const_specs = [_const_spec((D_MODEL, D_MODEL))] + const_specs
    (yp,), (ys,) = _rowwise_call(
        body, name, p_ins, s_ins, consts, const_specs,
        [_row_out(xp.shape[0], D_MODEL, F32)], [_row_out(xs.shape[0], D_MODEL, F32)])
    return yp, ys


def _conv_kernel(x_ref, hist_ref, g_ref, w1_ref, b1_ref, wdw_ref, bdw_ref,
                 lng_ref, lnb_ref, w2_ref, b2_ref, o_ref, hist_out_ref,
                 shift_ref, c_ref, *, nb, t, carry):
    i = pl.program_id(0)
    length = CONV_PAD + t
    x = x_ref[...]
    h = _rms(x, g_ref[...]).astype(BF16)
    u = _dot(h, w1_ref[...]) + b1_ref[...]
    glu = u[:, :CONV_C] * jax.nn.sigmoid(u[:, CONV_C:])

    lead = CONV_PAD - CONV_HIST
    if carry:
        @pl.when(i == 0)
        def _():
            shift_ref[0, :, lead:CONV_PAD, :] = hist_ref[...]
    else:
        shift_ref[0, :, lead:CONV_PAD, :] = hist_ref[...]
    shift_ref[0, :, 0:lead, :] = jnp.zeros((nb, lead, CONV_C), F32)
    shift_ref[0, :, CONV_PAD:, :] = glu.reshape(nb, t, CONV_C)
    for s in range(1, SUBLANES):
        shift_ref[s, :, 0:length - SUBLANES, :] = shift_ref[0, :, s:s + length - SUBLANES, :]

    for lt in range(CONV_C // LANES):
        lanes = slice(lt * LANES, (lt + 1) * LANES)
        tap_w = [wdw_ref[w, :, lanes] for w in range(CONV_W)]
        bias = bdw_ref[:, lanes]

        def conv_rows(b, j, lanes=lanes, tap_w=tap_w, bias=bias):
            base = pl.multiple_of(j * CONV_ROWS, CONV_ROWS)
            frames = {}

            def window(off):
                if off not in frames:
                    s = off % SUBLANES
                    frames[off] = shift_ref[s, b, pl.ds(base + (off - s), SUBLANES), lanes]
                return frames[off]

            chains = CONV_ROWS // SUBLANES
            acc = [[bias, None] for _ in range(chains)]
            for off in range(lead, lead + CONV_W + (chains - 1) * SUBLANES):
                for k in range(chains):
                    w = off - lead - k * SUBLANES
                    if 0 <= w < CONV_W:
                        term = window(off) * tap_w[w]
                        part = acc[k][w % 2]
                        acc[k][w % 2] = term if part is None else part + term
            for k in range(chains):
                rows = pl.ds(pl.multiple_of(b * t + base + k * SUBLANES, SUBLANES), SUBLANES)
                c_ref[rows, lanes] = acc[k][0] + acc[k][1]

        def conv_stream(b, carry_val, conv_rows=conv_rows):
            def conv_step(j, c):
                conv_rows(b, j)
                return c
            return lax.fori_loop(0, t // CONV_ROWS, conv_step, carry_val)

        lax.fori_loop(0, nb, conv_stream, 0)

    c = c_ref[...]
    mu = jnp.mean(c, axis=-1, keepdims=True)
    cc = c - mu
    var = jnp.mean(cc * cc, axis=-1, keepdims=True)
    y = cc * lax.rsqrt(var + EPS) * lng_ref[...] + lnb_ref[...]
    y = (y * jax.nn.sigmoid(y)).astype(BF16)
    o_ref[...] = x + _dot(y, w2_ref[...]) + b2_ref[...]

    tail = shift_ref[0, :, length - CONV_HIST:length, :]
    hist_out_ref[...] = tail
    if carry:
        shift_ref[0, :, lead:CONV_PAD, :] = tail


def _conv(x, hist, g, w1, b1, wdw, bdw, lng, lnb, w2, b2, *, nb, t):
    rows = x.shape[0]
    n_streams = hist.shape[0]
    tm = nb * t
    carry = nb == 1 and rows // n_streams > t
    if carry:
        assert n_streams == 1
        hist_map = lambda i: (0, 0, 0)
    else:
        assert rows == n_streams * t
        hist_map = lambda i: (i, 0, 0)
    vec = lambda n: _const_spec((1, n))
    return pl.pallas_call(
        functools.partial(_conv_kernel, nb=nb, t=t, carry=carry),
        grid=(rows // tm,),
        in_specs=[_rows_spec(tm, D_MODEL),
                  pl.BlockSpec((nb, CONV_HIST, CONV_C), hist_map),
                  vec(D_MODEL), _const_spec((D_MODEL, 2 * CONV_C)), vec(2 * CONV_C),
                  _const_spec((CONV_W, SUBLANES, CONV_C)), _const_spec((SUBLANES, CONV_C)),
                  vec(CONV_C), vec(CONV_C),
                  _const_spec((CONV_C, D_MODEL)), vec(D_MODEL)],
        out_specs=[_rows_spec(tm, D_MODEL),
                   pl.BlockSpec((nb, CONV_HIST, CONV_C), hist_map)],
        out_shape=[jax.ShapeDtypeStruct((rows, D_MODEL), F32),
                   jax.ShapeDtypeStruct((n_streams, CONV_HIST, CONV_C), F32)],
        scratch_shapes=[pltpu.VMEM((SUBLANES, nb, CONV_PAD + t, CONV_C), F32),
                        pltpu.VMEM((tm, CONV_C), F32)],
        compiler_params=_params(1), name="conv_module",
    )(x, hist, g, w1, b1, wdw, bdw, lng, lnb, w2, b2)


def _kv_rows(ins, consts, outs):
    (x_ref,), (g_ref, wk_ref, wv_ref, kg_ref) = ins, consts
    h = _rms(x_ref[...], g_ref[...]).astype(BF16)
    k = _group_rms(_dot(h, wk_ref[...]), kg_ref[...])
    v = _dot(h, wv_ref[...])
    outs[0][...] = k
    outs[1][...] = v
    if len(outs) == 3:
        vt_ref = outs[2]
        tm = v.shape[0]
        vt_ref[:, 0, :HEAD_W, :] = v.T.astype(BF16).reshape(N_HEADS, HEAD_W, tm)
        vt_ref[:, 0, HEAD_W:, :] = jnp.ones((N_HEADS, VT_ROWS - HEAD_W, tm), BF16)


def _kv(xp, xs, g, wk, wv, kg):
    qk = N_HEADS * HEAD_W
    n_p = xp.shape[0] // ROW_TILE
    vt_out = (jax.ShapeDtypeStruct((N_HEADS, n_p, VT_ROWS, ROW_TILE), BF16),
              (N_HEADS, 1, VT_ROWS, ROW_TILE), lambda t: (0, t, 0, 0))
    return _rowwise_call(
        _kv_rows, "kv_proj", [xp], [xs], [g, wk, wv, kg],
        [_const_spec((1, D_MODEL)), _const_spec((D_MODEL, qk)), _const_spec((D_MODEL, qk)),
         _const_spec((1, qk))],
        [_row_out(xp.shape[0], qk, F32), _row_out(xp.shape[0], qk, F32), vt_out],
        [_row_out(xs.shape[0], qk, F32), _row_out(xs.shape[0], qk, F32)])


def _q_rows(ins, consts, outs):
    (x_ref,), (g_ref, wq_ref, qg_ref), (q_ref,) = ins, consts, outs
    h = _rms(x_ref[...], g_ref[...]).astype(BF16)
    q = _group_rms(_dot(h, wq_ref[...]), qg_ref[...]) * (SCALE * LOG2E)
    q_ref[...] = q.astype(BF16)


def _q(xp, xs, g, wq, qg):
    qk = N_HEADS * HEAD_W
    (qp,), (qs,) = _rowwise_call(
        _q_rows, "q_proj", [xp], [xs], [g, wq, qg],
        [_const_spec((1, D_MODEL)), _const_spec((D_MODEL, qk)), _const_spec((1, qk))],
        [_row_out(xp.shape[0], qk, BF16)], [_row_out(xs.shape[0], qk, BF16)])
    return qp, qs


def _lambda(lq1_ref, lk1_ref, lq2_ref, lk2_ref, lam_init):
    s1 = jnp.sum(lq1_ref[...] * lk1_ref[...], axis=-1, keepdims=True)
    s2 = jnp.sum(lq2_ref[...] * lk2_ref[...], axis=-1, keepdims=True)
    return jnp.exp(s1) - jnp.exp(s2) + lam_init


def _attn_prompt_kernel(q_ref, k_ref, vt_ref, lq1_ref, lk1_ref, lq2_ref, lk2_ref,
                        sg_ref, o_ref, qbd_ref, s0_ref, s1_ref, mb0_ref, mb1_ref,
                        p0_ref, p1_ref, alpha0_ref, alpha1_ref,
                        m_ref, acc_ref, *, tq, tk, lam_init):
    qi = pl.program_id(1)
    q_t = q_ref[...].astype(F32).T
    row = lax.broadcasted_iota(jnp.int32, (HEAD_W, 2 * tq), 0)
    col = lax.broadcasted_iota(jnp.int32, (HEAD_W, 2 * tq), 1)
    qq = jnp.concatenate([q_t, q_t], axis=1)
    qbd_ref[...] = jnp.where((row < HEAD_DIM) == (col < tq), qq, 0.0).astype(BF16)
    m_ref[...] = jnp.full(m_ref.shape, NEG, F32)
    acc_ref[...] = jnp.zeros(acc_ref.shape, F32)
    bufs = ((s0_ref, mb0_ref), (s1_ref, mb1_ref))
    pbufs = ((p0_ref, alpha0_ref), (p1_ref, alpha1_ref))
    causal = (lax.broadcasted_iota(jnp.int32, (tk, 1), 0) // CHUNK
              <= lax.broadcasted_iota(jnp.int32, (1, tk), 1) // CHUNK)

    def scores(j, buf, diagonal=None):
        s_ref, mb_ref = buf
        kb = k_ref[pl.ds(pl.multiple_of(j * tk, tk), tk), :].astype(BF16)
        s = _dot(kb, qbd_ref[...])
        if diagonal is None:
            s_ref[...] = s
            mb_ref[...] = jnp.max(s, axis=0, keepdims=True)
            return
        for g in range(2 * tq // tk):
            cols = slice(g * tk, (g + 1) * tk)
            group = g % (tq // tk)
            piece = s[:, cols]
            if group < diagonal:
                piece = jnp.full_like(piece, NEG)
            elif group == diagonal:
                piece = jnp.where(causal, piece, NEG)
            s_ref[:, cols] = piece
            mb_ref[:, cols] = jnp.max(piece, axis=0, keepdims=True)

    strips = [slice(c * MXU_DIM, (c + 1) * MXU_DIM) for c in range(2 * tq // MXU_DIM)]

    def softmax(buf, pbuf):
        s_ref, mb_ref = buf
        p_ref, alpha_ref = pbuf
        for cols in strips:
            m_old = m_ref[:, cols]
            m_new = jnp.maximum(m_old, mb_ref[:, cols])
            alpha_ref[:, cols] = jnp.exp2(m_old - m_new)
            p_ref[:, cols] = jnp.exp2(s_ref[:, cols] - m_new).astype(BF16)
            m_ref[:, cols] = m_new

    def accumulate(j, pbuf):
        p_ref, alpha_ref = pbuf
        vt = vt_ref[0, j]
        for cols in strips:
            acc_ref[:, cols] = acc_ref[:, cols] * alpha_ref[:, cols] + _dot(vt, p_ref[:, cols])

    assert tq == 2 * tk
    d0 = qi * (tq // tk)
    scores(d0, bufs[0], diagonal=0)
    scores(d0 + 1, bufs[1], diagonal=1)
    softmax(bufs[0], pbufs[0])

    def two_steps(i, carry):
        even_block = jnp.where(i == 0, d0, 2 * i - 2)
        scores(2 * i, bufs[0])
        softmax(bufs[1], pbufs[1])
        accumulate(even_block, pbufs[0])
        scores(2 * i + 1, bufs[1])
        softmax(bufs[0], pbufs[0])
        accumulate(even_block + 1, pbufs[1])
        return carry

    lax.fori_loop(0, qi, two_steps, 0)
    last_even = jnp.where(qi == 0, d0, 2 * qi - 2)
    softmax(bufs[1], pbufs[1])
    accumulate(last_even, pbufs[0])
    accumulate(last_even + 1, pbufs[1])

    a = acc_ref[:HEAD_W, :2 * tq] * (1.0 / acc_ref[HEAD_W:HEAD_W + 1, :2 * tq])
    lam = _lambda(lq1_ref, lk1_ref, lq2_ref, lk2_ref, lam_init)
    o_t = a[:, :tq] - lam * a[:, tq:]
    ms = jnp.mean(o_t * o_t, axis=0, keepdims=True)
    o_t = o_t * lax.rsqrt(ms + EPS) * sg_ref[...] * (1.0 - lam_init)
    o_ref[...] = o_t.T.astype(BF16)


def _attn_prompt(q, kb, vt, lams, sg_col, lam_init):
    seq = q.shape[0]
    tq, tk = ATTN_Q_TILE, ATTN_K_TILE
    assert vt.shape == (N_HEADS, seq // tk, VT_ROWS, tk)
    lam_spec = pl.BlockSpec((1, HEAD_DIM), lambda h, i: (0, 0))
    stat = pltpu.VMEM((1, 2 * tq), F32)
    width = 2 * tq
    return pl.pallas_call(
        functools.partial(_attn_prompt_kernel, tq=tq, tk=tk, lam_init=lam_init),
        grid=(N_HEADS, seq // tq),
        in_specs=[pl.BlockSpec((tq, HEAD_W), lambda h, i: (i, h)),
                  pl.BlockSpec((seq, HEAD_W), lambda h, i: (0, h)),
                  pl.BlockSpec((1, seq // tk, VT_ROWS, tk), lambda h, i: (h, 0, 0, 0)),
                  lam_spec, lam_spec, lam_spec, lam_spec,
                  pl.BlockSpec((HEAD_W, 1), lambda h, i: (0, 0))],
        out_specs=pl.BlockSpec((tq, HEAD_W), lambda h, i: (i, h)),
        out_shape=jax.ShapeDtypeStruct((seq, N_HEADS * HEAD_W), BF16),
        scratch_shapes=[pltpu.VMEM((HEAD_W, 2 * tq), BF16),
                        pltpu.VMEM((tk, width), F32),
                        pltpu.VMEM((tk, width), F32),
                        stat, stat,
                        pltpu.VMEM((tk, width), BF16),
                        pltpu.VMEM((tk, width), BF16),
                        stat, stat,
                        stat,
                        pltpu.VMEM((VT_ROWS, width), F32)],
        compiler_params=_params(2), name="attn_prompt",
    )(q, kb, vt, *lams, sg_col)


def _attn_sample_kernel(q_ref, ck_ref, cv_ref, kn_ref, vn_ref, lq1_ref, lk1_ref,
                        lq2_ref, lk2_ref, sg_ref, o_ref, *, t, lam_init):
    row = lax.broadcasted_iota(jnp.int32, (2 * t, HEAD_W), 0)
    col = lax.broadcasted_iota(jnp.int32, (2 * t, HEAD_W), 1)
    block_diag = (row < t) == (col < HEAD_DIM)
    nt = (((1,), (1,)), ((), ()))
    past = cv_ref.shape[1] // N_HEADS
    lam = _lambda(lq1_ref, lk1_ref, lq2_ref, lk2_ref, lam_init)
    for h in range(N_HEADS):
        lanes = slice(h * HEAD_W, (h + 1) * HEAD_W)
        q = q_ref[:, lanes]
        qq = jnp.concatenate([q, q], axis=0)
        qbd = jnp.where(block_diag, qq, jnp.zeros_like(qq))
        s_c = _dot(qbd, ck_ref[0, lanes, :].astype(BF16))
        s_n = lax.dot_general(qbd, kn_ref[:, lanes].astype(BF16), nt,
                              preferred_element_type=F32)
        m = jnp.maximum(jnp.max(s_c, axis=-1, keepdims=True),
                        jnp.max(s_n, axis=-1, keepdims=True))
        p_c = jnp.exp2(s_c - m)
        p_n = jnp.exp2(s_n - m)
        l = jnp.sum(p_c, axis=-1, keepdims=True) + jnp.sum(p_n, axis=-1, keepdims=True)
        v_c = cv_ref[0, pl.ds(h, past, stride=N_HEADS), :]
        a = (_dot(p_c.astype(BF16), v_c.astype(BF16))
             + _dot(p_n.astype(BF16), vn_ref[:, lanes].astype(BF16))) / l
        o = a[:t] - lam * a[t:]
        o = _rms(o, sg_ref[...]) * (1.0 - lam_init)
        o_ref[:, lanes] = o.astype(BF16)


def _attn_sample(q, cache_kt, cache_v, k_new, v_new, lams, sg_row, lam_init, *, t):
    n_streams, width, past = cache_kt.shape
    new_spec = pl.BlockSpec((t, width), lambda b: (b, 0))
    lam_spec = pl.BlockSpec((1, HEAD_DIM), lambda b: (0, 0))
    return pl.pallas_call(
        functools.partial(_attn_sample_kernel, t=t, lam_init=lam_init),
        grid=(n_streams,),
        in_specs=[new_spec,
                  pl.BlockSpec((1, width, past), lambda b: (b, 0, 0)),
                  pl.BlockSpec((1, past * N_HEADS, HEAD_W), lambda b: (b, 0, 0)),
                  new_spec, new_spec,
                  lam_spec, lam_spec, lam_spec, lam_spec,
                  pl.BlockSpec((1, HEAD_W), lambda b: (0, 0))],
        out_specs=new_spec,
        out_shape=jax.ShapeDtypeStruct((n_streams * t, width), BF16),
        compiler_params=_params(1), name="attn_sample",
    )(q, cache_kt, cache_v, k_new, v_new, *lams, sg_row)


def kernel(x_prompt, x_sample, cache_k, cache_v, state_conv, ffn_norm, ffn_w_gate, ffn_w_up, ffn_w_down, conv_norm, conv_w_pw1, conv_b_pw1, conv_w_dw, conv_b_dw, conv_ln_g, conv_ln_b, conv_w_pw2, conv_b_pw2, kv_norm, w_k, w_v, k_norm, attn_norm, w_q, q_norm, lambda_q1, lambda_k1, lambda_q2, lambda_k2, subln, w_o):
    batch, seq, _ = x_prompt.shape
    dec_batch, dec_seq, _ = x_sample.shape
    assert batch == 1 and seq % ATTN_Q_TILE == 0 and dec_seq % CONV_ROWS == 0
    assert ffn_norm.shape[0] == 2 and conv_norm.shape[0] == 1 and attn_norm.shape[0] == 1
    assert dec_batch * dec_seq == ROW_TILE
    past = cache_k.shape[1]
    qk = N_HEADS * HEAD_W
    row = lambda v: v.reshape(1, -1).astype(F32)
    bf = lambda w: w.astype(BF16)

    ffn_params = (ffn_norm.astype(F32)[:, :, None, :], bf(ffn_w_gate), bf(ffn_w_up), bf(ffn_w_down))
    sub = lambda v: jnp.broadcast_to(v.astype(F32)[..., None, :],
                                     v.shape[:-1] + (SUBLANES, v.shape[-1]))
    conv_w = (row(conv_norm[0]), bf(conv_w_pw1[0]), row(conv_b_pw1[0]), sub(conv_w_dw[0]),
              sub(conv_b_dw[0]), row(conv_ln_g[0]), row(conv_ln_b[0]), bf(conv_w_pw2[0]),
              row(conv_b_pw2[0]))
    kv_w = (row(kv_norm), bf(w_k), bf(w_v), row(jnp.tile(k_norm, qk // HEAD_DIM)))
    q_w = (row(attn_norm[0]), bf(w_q[0]), row(jnp.tile(q_norm[0], qk // HEAD_DIM)))
    lams = (row(lambda_q1[0]), row(lambda_k1[0]), row(lambda_q2[0]), row(lambda_k2[0]))
    lam_init = 0.8 - 0.6 * math.exp(-0.3 * 1)
    sg = subln[0].astype(F32)
    wo = bf(w_o[0])

    xp = x_prompt.reshape(seq, D_MODEL)
    xs = x_sample.reshape(dec_batch * dec_seq, D_MODEL)

    xp, xs = _ffn(xp, xs, ffn_params, 0, 0)
    zero_hist = jnp.zeros((1, CONV_HIST, CONV_C), F32)
    xp, hist_p = _conv(xp, zero_hist, *conv_w, nb=1, t=ROW_TILE)
    xs, hist_s = _conv(xs, state_conv[0], *conv_w, nb=8, t=dec_seq)
    xp, xs = _ffn(xp, xs, ffn_params, 0, 1)
    (k_p, v_p, vt_p), (k_s, v_s) = _kv(xp, xs, *kv_w)

    xp, xs = _ffn(xp, xs, ffn_params, 1, 0)
    q_p, q_s = _q(xp, xs, *q_w)
    a_p = _attn_prompt(q_p, k_p, vt_p, lams, sg.reshape(HEAD_W, 1), lam_init)
    cache_kt = jnp.transpose(cache_k, (0, 2, 3, 4, 1)).reshape(dec_batch, qk, past)
    a_s = _attn_sample(q_s, cache_kt, cache_v.reshape(dec_batch, past * N_HEADS, HEAD_W),
                       k_s, v_s, lams, sg.reshape(1, HEAD_W), lam_init, t=dec_seq)
    yp, ys = _ffn(xp, xs, ffn_params, 1, 1, attn=(a_p, a_s), wo=wo)

    return (yp.reshape(1, seq, D_MODEL),
            ys.reshape(dec_batch, dec_seq, D_MODEL),
            k_p.reshape(1, seq, N_HEADS, 2, HEAD_DIM),
            v_p.reshape(1, seq, N_HEADS, HEAD_W),
            hist_p.reshape(1, 1, CONV_HIST, CONV_C),
            k_s.reshape(dec_batch, dec_seq, N_HEADS, 2, HEAD_DIM),
            v_s.reshape(dec_batch, dec_seq, N_HEADS, HEAD_W),
            hist_s.reshape(1, dec_batch, CONV_HIST, CONV_C))
```
